```python
import jax, jax.numpy as jnp
from jax import lax
import numpy as np

D_MODEL = 2048
BATCH = 16
SEQ = 2048
DEPTH = 1
DEC_BATCH = 8
DEC_SEQ = 4096
PAST_LEN = 128

GRID_W = 64
POOL_WINDOWS = (2, 4, 8, 16)
POOL_GROUPS = len(POOL_WINDOWS)
POOL_WIDTH = D_MODEL // 2
POOL_GROUP_W = POOL_WIDTH // POOL_GROUPS
HEAD_DIM = 64
N_HEADS = (D_MODEL // 2) // HEAD_DIM
ATTN_WIDTH = N_HEADS * HEAD_DIM
WIN_R = 8
WIN_C = 16
IN_SPLITS = (POOL_WIDTH, POOL_WIDTH, ATTN_WIDTH, ATTN_WIDTH, ATTN_WIDTH, ATTN_WIDTH, D_MODEL, D_MODEL)
IN_WIDTH = sum(IN_SPLITS)
NORM_EPS = 1e-6

kernel_name = "gated_pool_natten_hybrid_encoder"


def _rms(x, g):
    xf = x.astype(jnp.float32)
    y = xf * lax.rsqrt(jnp.mean(xf * xf, axis=-1, keepdims=True) + NORM_EPS) * g.astype(jnp.float32)
    return y.astype(x.dtype)


def _pool_mixer(u, pool_w, pool_scale):
    B, S, _ = u.shape
    uf = u.astype(jnp.float32)
    cs = jnp.concatenate([jnp.zeros((B, 1, POOL_WIDTH), jnp.float32), jnp.cumsum(uf, axis=1)], axis=1)
    t = jnp.arange(S)
    outs = []
    for gi, w in enumerate(POOL_WINDOWS):
        lo = jnp.clip(t - w // 2, 0, S - 1)
        hi = jnp.clip(t + w // 2 - 1, 0, S - 1)
        sl = slice(gi * POOL_GROUP_W, (gi + 1) * POOL_GROUP_W)
        csg = cs[:, :, sl]
        cnt = (hi - lo + 1).astype(jnp.float32)[None, :, None]
        mean = (jnp.take(csg, hi + 1, axis=1) - jnp.take(csg, lo, axis=1)) / cnt
        outs.append(mean - uf[:, :, sl])
    d = jnp.stack(outs, axis=2).astype(u.dtype)
    y = jnp.einsum('bsgc,gcd->bsgd', d, pool_w).reshape(B, S, POOL_WIDTH)
    return y * pool_scale


def _neighbourhood_attention(q, k, v, rpb):
    B, S, H, hd = q.shape
    rows = S // GRID_W
    kr = min(WIN_R, rows)
    kc = WIN_C
    n_keys = kr * kc
    r = jnp.arange(rows)
    c = jnp.arange(GRID_W)
    r_start = jnp.clip(r - kr // 2, 0, rows - kr)
    c_start = jnp.clip(c - kc // 2, 0, GRID_W - kc)
    key_rows = r_start[:, None] + jnp.arange(kr)[None, :]
    key_cols = c_start[:, None] + jnp.arange(kc)[None, :]
    idx = (key_rows[:, None, :, None] * GRID_W + key_cols[None, :, None, :]).reshape(rows, GRID_W, n_keys)
    dr = key_rows - r[:, None] + (WIN_R - 1)
    dc = key_cols - c[:, None] + (WIN_C - 1)
    bias = rpb[:, dr[:, None, :, None], dc[None, :, None, :]]
    bias = bias.reshape(H, rows, GRID_W, n_keys).transpose(1, 0, 2, 3)
    q_rows = q.reshape(B, rows, GRID_W, H, hd).transpose(1, 0, 2, 3, 4)
    scale = HEAD_DIM ** -0.5

    def row_block(args):
        q_r, idx_r, bias_r = args
        k_win = jnp.take(k, idx_r, axis=1)
        v_win = jnp.take(v, idx_r, axis=1)
        s = jnp.einsum('bwhd,bwnhd->bhwn', q_r, k_win, preferred_element_type=jnp.float32)
        s = s * scale + bias_r[None].astype(jnp.float32)
        p = jax.nn.softmax(s, axis=-1).astype(v.dtype)
        return jnp.einsum('bhwn,bwnhd->bwhd', p, v_win)

    o = lax.map(row_block, (q_rows, idx, bias))
    return o.transpose(1, 0, 2, 3, 4).reshape(B, S, H * hd)


def _layer(x, c, w_ada, b_ada, norm_g, w_in, pool_w, pool_scale, q_norm_g, k_norm_g,
           rpb, w_pool_up, w_attn_up, w_o):
    B, S, D = x.shape
    ada = jax.nn.silu(c) @ w_ada + b_ada
    shift, scl, gate = jnp.split(ada, 3, axis=-1)
    h = _rms(x, norm_g) * (1 + scl[:, None, :]) + shift[:, None, :]
    proj = h @ w_in
    cuts = [int(v) for v in np.cumsum(IN_SPLITS)[:-1]]
    pool_u, pool_z, q, k, v, attn_z, g_pool, g_attn = jnp.split(proj, cuts, axis=-1)
    pool_y = _pool_mixer(pool_u, pool_w, pool_scale) * jax.nn.silu(pool_z)
    q = _rms(q.reshape(B, S, N_HEADS, HEAD_DIM), q_norm_g)
    k = _rms(k.reshape(B, S, N_HEADS, HEAD_DIM), k_norm_g)
    v = v.reshape(B, S, N_HEADS, HEAD_DIM)
    attn_y = _neighbourhood_attention(q, k, v, rpb) * jax.nn.silu(attn_z)
    merged = jax.nn.sigmoid(g_pool) * (pool_y @ w_pool_up) + jax.nn.sigmoid(g_attn) * (attn_y @ w_attn_up)
    return x + gate[:, None, :] * (merged @ w_o)


def setup_inputs(seed: int = 0) -> dict:
    key = jax.random.key(seed)
    ks = jax.random.split(key, 16)
    f32 = jnp.float32
    nrm = lambda k, shp, s: jax.random.normal(k, shp, f32) * s
    return {
        "x_prompt": nrm(ks[0], (BATCH, SEQ, D_MODEL), 1.0),
        "x_sample": nrm(ks[1], (DEC_BATCH, DEC_SEQ, D_MODEL), 1.0),
        "c_prompt": nrm(ks[2], (BATCH, D_MODEL), 1.0),
        "c_sample": nrm(ks[3], (DEC_BATCH, D_MODEL), 1.0),
        "w_ada": nrm(ks[4], (DEPTH, D_MODEL, 3 * D_MODEL), 0.5 * D_MODEL ** -0.5),
        "b_ada": nrm(ks[5], (DEPTH, 3 * D_MODEL), 0.01),
        "norm_g": 1.0 + nrm(ks[6], (DEPTH, D_MODEL), 0.02),
        "w_in": nrm(ks[7], (DEPTH, D_MODEL, IN_WIDTH), D_MODEL ** -0.5),
        "pool_w": nrm(ks[8], (DEPTH, POOL_GROUPS, POOL_GROUP_W, POOL_GROUP_W), POOL_GROUP_W ** -0.5),
        "pool_scale": 1.0 + nrm(ks[9], (DEPTH, POOL_WIDTH), 0.02),
        "q_norm_g": 1.0 + nrm(ks[10], (DEPTH, HEAD_DIM), 0.02),
        "k_norm_g": 1.0 + nrm(ks[11], (DEPTH, HEAD_DIM), 0.02),
        "rpb": nrm(ks[12], (DEPTH, N_HEADS, 2 * WIN_R - 1, 2 * WIN_C - 1), 0.02),
        "w_pool_up": nrm(ks[13], (DEPTH, POOL_WIDTH, D_MODEL), POOL_WIDTH ** -0.5),
        "w_attn_up": nrm(ks[14], (DEPTH, ATTN_WIDTH, D_MODEL), ATTN_WIDTH ** -0.5),
        "w_o": nrm(ks[15], (DEPTH, D_MODEL, D_MODEL), D_MODEL ** -0.5),
    }


def reference(x_prompt, x_sample, c_prompt, c_sample, w_ada, b_ada, norm_g, w_in, pool_w,
              pool_scale, q_norm_g, k_norm_g, rpb, w_pool_up, w_attn_up, w_o):
    def trunk(x, c):
        for l in range(DEPTH):
            x = _layer(x, c, w_ada[l], b_ada[l], norm_g[l], w_in[l], pool_w[l], pool_scale[l],
                       q_norm_g[l], k_norm_g[l], rpb[l], w_pool_up[l], w_attn_up[l], w_o[l])
        return x
    y_prompt = trunk(x_prompt, c_prompt)
    y_sample = trunk(x_sample, c_sample)
    return (y_prompt, y_sample)
```

```python
import functools

import jax
import jax.numpy as jnp
from jax import lax
from jax.experimental import pallas as pl
from jax.experimental.pallas import tpu as pltpu

D_MODEL = 2048
GRID_W = 64
POOL_WINDOWS = (2, 4, 8, 16)
POOL_WIDTH = 1024
GROUP_W = 256
N_GROUPS = 4
HEAD_DIM = 64
N_HEADS = 16
ATTN_WIDTH = 1024
WIN_R = 8
WIN_C = 16
IN_WIDTH = 10240
NORM_EPS = 1e-6
MASK_BIAS = -1e30

COL_POOL_U, COL_POOL_Z, COL_Q, COL_K, COL_V, COL_ATTN_Z, COL_G_POOL, COL_G_ATTN = (
    0, 1024, 2048, 3072, 4096, 5120, 6144, 8192)

VMEM_LIMIT_BYTES = 52 * 1024 * 1024

BF16 = jnp.bfloat16
F32 = jnp.float32


def _silu(x):
    return x * jax.nn.sigmoid(x)


ADA_TN = 512


def _ada_kernel(c_ref, w_ref, b_ref, o_ref):
    c = c_ref[...]
    a = _silu(c).astype(BF16)
    o_ref[...] = jnp.dot(a, w_ref[...].astype(BF16), preferred_element_type=F32) + b_ref[...]


def _ada(c_all, w_ada, b_ada):
    nb = c_all.shape[0]
    n = w_ada.shape[1]
    return pl.pallas_call(
        _ada_kernel,
        grid=(n // ADA_TN,),
        in_specs=[
            pl.BlockSpec((nb, D_MODEL), lambda j: (0, 0)),
            pl.BlockSpec((D_MODEL, ADA_TN), lambda j: (0, j)),
            pl.BlockSpec((1, ADA_TN), lambda j: (0, j)),
        ],
        out_specs=pl.BlockSpec((nb, ADA_TN), lambda j: (0, j)),
        out_shape=jax.ShapeDtypeStruct((nb, n), F32),
        compiler_params=pltpu.CompilerParams(dimension_semantics=("parallel",)),
        name="ada",
    )(c_all, w_ada, b_ada.reshape(1, n))


IN_TM = 1024
IN_TN = 1024
IN_NORM_CHUNK = 64


def _head_mean_square(acc):
    r = lax.broadcasted_iota(jnp.int32, (256, 256), 0) // HEAD_DIM
    c = lax.broadcasted_iota(jnp.int32, (256, 256), 1) // HEAD_DIM
    seg = (r == c).astype(BF16)
    sq = (acc * acc).astype(BF16)
    parts = [jnp.dot(sq[:, i * 256:(i + 1) * 256], seg, preferred_element_type=F32)
             for i in range(IN_TN // 256)]
    return jnp.concatenate(parts, axis=1) * (1.0 / HEAD_DIM)


def _in_proj_kernel(x_ref, shift_ref, scl_ref, g_ref, w_ref, qg_ref, kg_ref, o_ref, h_ref):
    j = pl.program_id(1)

    @pl.when(j == 0)
    def _():
        g = g_ref[...]
        mod = 1.0 + scl_ref[...]
        shift = shift_ref[...]

        def body(i, carry):
            r0 = pl.multiple_of(i * IN_NORM_CHUNK, IN_NORM_CHUNK)
            x = x_ref[pl.ds(r0, IN_NORM_CHUNK), :]
            ms = jnp.mean(x * x, axis=-1, keepdims=True)
            y = x * lax.rsqrt(ms + NORM_EPS) * g
            h_ref[pl.ds(r0, IN_NORM_CHUNK), :] = (y * mod + shift).astype(BF16)
            return carry

        lax.fori_loop(0, IN_TM // IN_NORM_CHUNK, body, 0)

    acc = jnp.dot(h_ref[...], w_ref[...], preferred_element_type=F32)

    is_plain = (j == COL_POOL_U // IN_TN) | (j == COL_V // IN_TN)
    is_silu = (j == COL_POOL_Z // IN_TN) | (j == COL_ATTN_Z // IN_TN)

    @pl.when(is_plain)
    def _():
        o_ref[...] = acc.astype(BF16)

    @pl.when(is_silu)
    def _():
        o_ref[...] = _silu(acc).astype(BF16)

    @pl.when(j == COL_Q // IN_TN)
    def _():
        ms = _head_mean_square(acc)
        o_ref[...] = (acc * lax.rsqrt(ms + NORM_EPS) * qg_ref[...]).astype(BF16)

    @pl.when(j == COL_K // IN_TN)
    def _():
        ms = _head_mean_square(acc)
        o_ref[...] = (acc * lax.rsqrt(ms + NORM_EPS) * kg_ref[...]).astype(BF16)

    @pl.when(j >= COL_G_POOL // IN_TN)
    def _():
        o_ref[...] = jax.nn.sigmoid(acc).astype(BF16)


def _in_proj(x2, ada3, b_off, seq, norm_g, w_in_bf, qg_t, kg_t):
    m = x2.shape[0]
    bpb = seq // IN_TM
    return pl.pallas_call(
        _in_proj_kernel,
        grid=(m // IN_TM, IN_WIDTH // IN_TN),
        in_specs=[
            pl.BlockSpec((IN_TM, D_MODEL), lambda i, j: (i, 0)),
            pl.BlockSpec((None, 1, D_MODEL), lambda i, j: (b_off + i // bpb, 0, 0)),
            pl.BlockSpec((None, 1, D_MODEL), lambda i, j: (b_off + i // bpb, 0, 1)),
            pl.BlockSpec((1, D_MODEL), lambda i, j: (0, 0)),
            pl.BlockSpec((D_MODEL, IN_TN), lambda i, j: (0, j)),
            pl.BlockSpec((1, IN_TN), lambda i, j: (0, 0)),
            pl.BlockSpec((1, IN_TN), lambda i, j: (0, 0)),
        ],
        out_specs=pl.BlockSpec((IN_TM, IN_TN), lambda i, j: (i, j)),
        out_shape=jax.ShapeDtypeStruct((m, IN_WIDTH), BF16),
        scratch_shapes=[pltpu.VMEM((IN_TM, D_MODEL), BF16)],
        compiler_params=pltpu.CompilerParams(
            dimension_semantics=("parallel", "arbitrary"), vmem_limit_bytes=VMEM_LIMIT_BYTES),
        name="in_proj",
    )(x2, ada3, ada3, norm_g, w_in_bf, qg_t, kg_t)


KEYS_PER_ROW_BLOCK = WIN_R * GRID_W


def _attn_kernel(q_ref, k_ref, v_ref, z_ref, tbl_ref, o_ref, *, rows):
    lane = lax.broadcasted_iota(jnp.int32, (GRID_W, 128), 1)
    first_head = lane < HEAD_DIM

    def row_body(r, carry):
        rs = jnp.clip(r - WIN_R // 2, 0, rows - WIN_R)
        dr0 = rs - r + (WIN_R - 1)
        q0 = pl.multiple_of(r * GRID_W, GRID_W)
        k0 = pl.multiple_of(rs * GRID_W, GRID_W)
        for hp in range(GROUP_W // 128):
            lanes = slice(hp * 128, (hp + 1) * 128)
            q2 = q_ref[pl.ds(q0, GRID_W), lanes]
            zero = jnp.zeros_like(q2)
            q_stack = jnp.concatenate(
                [jnp.where(first_head, q2, zero), jnp.where(first_head, zero, q2)], axis=0)
            k2 = k_ref[pl.ds(k0, KEYS_PER_ROW_BLOCK), lanes]
            v2 = v_ref[pl.ds(k0, KEYS_PER_ROW_BLOCK), lanes]
            s = lax.dot_general(q_stack, k2, (((1,), (1,)), ((), ())), preferred_element_type=F32)
            bias = jnp.concatenate(
                [jnp.concatenate([tbl_ref[dr0 + 2 * jj, 2 * hp + e] for jj in range(WIN_R // 2)], axis=1)
                 for e in range(2)], axis=0)
            s = s + bias
            m = jnp.max(s, axis=-1, keepdims=True)
            p = jnp.exp(s - m)
            l = jnp.sum(p, axis=-1, keepdims=True)
            pv = jnp.dot(p.astype(BF16), v2, preferred_element_type=F32)
            pv = pv / l
            o = jnp.where(first_head, pv[:GRID_W], pv[GRID_W:])
            o = o * z_ref[pl.ds(q0, GRID_W), lanes].astype(F32)
            o_ref[pl.ds(q0, GRID_W), lanes] = o.astype(BF16)
        return carry

    lax.fori_loop(0, rows, row_body, 0)


def _attn(proj, bias_tbl, n_batch, seq):
    rows = seq // GRID_W
    col = lambda off: (lambda b, g: (b, off // GROUP_W + g))
    blk = (seq, GROUP_W)
    return pl.pallas_call(
        functools.partial(_attn_kernel, rows=rows),
        grid=(n_batch, N_GROUPS),
        in_specs=[
            pl.BlockSpec(blk, col(COL_Q)),
            pl.BlockSpec(blk, col(COL_K)),
            pl.BlockSpec(blk, col(COL_V)),
            pl.BlockSpec(blk, col(COL_ATTN_Z)),
            pl.BlockSpec((2 * WIN_R - 2, N_HEADS // N_GROUPS, GRID_W, 128), lambda b, g: (0, g, 0, 0)),
        ],
        out_specs=pl.BlockSpec(blk, lambda b, g: (b, g)),
        out_shape=jax.ShapeDtypeStruct((n_batch * seq, ATTN_WIDTH), BF16),
        compiler_params=pltpu.CompilerParams(
            dimension_semantics=("parallel", "parallel"), vmem_limit_bytes=VMEM_LIMIT_BYTES),
        name="attn",
    )(proj, proj, proj, proj, bias_tbl)


def _bias_table(rpb):
    c = jnp.arange(GRID_W)
    c_start = jnp.clip(c - WIN_C // 2, 0, GRID_W - WIN_C)
    rel = c[None, :] - c[:, None] + (WIN_C - 1)
    in_win = (c[None, :] >= c_start[:, None]) & (c[None, :] < c_start[:, None] + WIN_C)
    t = jnp.where(in_win[None, None], rpb[:, :, jnp.clip(rel, 0, 2 * WIN_C - 2)], MASK_BIAS)
    pair = jnp.concatenate([t[:, :-1], t[:, 1:]], axis=-1)
    return pair.transpose(1, 0, 2, 3).astype(F32)


POOL_CHUNK = 256
POOL_PAD = 16


def _pool_kernel(u_ref, z_ref, w_ref, s_ref, o_ref, pad_ref, *, seq):
    g = pl.program_id(1)
    zeros = jnp.zeros((POOL_PAD, GROUP_W), F32)
    pad_ref[0:POOL_PAD, :] = zeros
    pad_ref[POOL_PAD + seq:POOL_PAD + seq + POOL_PAD, :] = zeros
    for ci in range(seq // POOL_CHUNK):
        t0 = ci * POOL_CHUNK
        pad_ref[POOL_PAD + t0:POOL_PAD + t0 + POOL_CHUNK, :] = u_ref[t0:t0 + POOL_CHUNK, :].astype(F32)

    w_mat = w_ref[...]
    scale = s_ref[...]
    for gi, win in enumerate(POOL_WINDOWS):
        half = win // 2

        @pl.when(g == gi)
        def _(win=win, half=half):
            for ci in range(seq // POOL_CHUNK):
                t0 = ci * POOL_CHUNK
                base = POOL_PAD + t0
                acc = pad_ref[base - half:base - half + POOL_CHUNK, :]
                for off in range(-half + 1, half):
                    acc = acc + pad_ref[base + off:base + off + POOL_CHUNK, :]
                t = t0 + lax.broadcasted_iota(jnp.int32, (POOL_CHUNK, 1), 0)
                lo = jnp.maximum(t - half, 0)
                hi = jnp.minimum(t + half - 1, seq - 1)
                cnt = (hi - lo + 1).astype(F32)
                d = acc / cnt - pad_ref[base:base + POOL_CHUNK, :]
                y = jnp.dot(d.astype(BF16), w_mat, preferred_element_type=F32)
                y = y * scale * z_ref[t0:t0 + POOL_CHUNK, :].astype(F32)
                o_ref[t0:t0 + POOL_CHUNK, :] = y.astype(BF16)


def _pool(proj, pool_w_bf, pool_scale2, n_batch, seq):
    blk = (seq, GROUP_W)
    return pl.pallas_call(
        functools.partial(_pool_kernel, seq=seq),
        grid=(n_batch, N_GROUPS),
        in_specs=[
            pl.BlockSpec(blk, lambda b, g: (b, COL_POOL_U // GROUP_W + g)),
            pl.BlockSpec(blk, lambda b, g: (b, COL_POOL_Z // GROUP_W + g)),
            pl.BlockSpec((None, GROUP_W, GROUP_W), lambda b, g: (g, 0, 0)),
            pl.BlockSpec((1, GROUP_W), lambda b, g: (0, g)),
        ],
        out_specs=pl.BlockSpec(blk, lambda b, g: (b, g)),
        out_shape=jax.ShapeDtypeStruct((n_batch * seq, POOL_WIDTH), BF16),
        scratch_shapes=[pltpu.VMEM((seq + 2 * POOL_PAD, GROUP_W), F32)],
        compiler_params=pltpu.CompilerParams(
            dimension_semantics=("parallel", "parallel"), vmem_limit_bytes=VMEM_LIMIT_BYTES),
        name="pool",
    )(proj, proj, pool_w_bf, pool_scale2)


OUT_TM = 256
OUT_CHUNK = 512


def _out_proj_kernel(x_ref, py_ref, ay_ref, gp_ref, ga_ref, gate_ref, wpu_ref, wau_ref, wo_ref, o_ref, m_ref):
    py = py_ref[...]
    ay = ay_ref[...]
    for ci in range(D_MODEL // OUT_CHUNK):
        cs = slice(ci * OUT_CHUNK, (ci + 1) * OUT_CHUNK)
        pooled = jnp.dot(py, wpu_ref[:, cs], preferred_element_type=F32)
        attended = jnp.dot(ay, wau_ref[:, cs], preferred_element_type=F32)
        merged = gp_ref[:, cs].astype(F32) * pooled + ga_ref[:, cs].astype(F32) * attended
        m_ref[:, cs] = merged.astype(BF16)
    y = jnp.dot(m_ref[...], wo_ref[...], preferred_element_type=F32)
    o_ref[...] = x_ref[...] + gate_ref[...] * y


def _out_proj(x2, pool_y, attn_y, proj, ada3, b_off, seq, wpu_bf, wau_bf, wo_bf):
    m = x2.shape[0]
    bpb = seq // OUT_TM
    resident = lambda shape: pl.BlockSpec(shape, lambda i: (0, 0), pipeline_mode=pl.Buffered(1))
    return pl.pallas_call(
        _out_proj_kernel,
        grid=(m // OUT_TM,),
        in_specs=[
            pl.BlockSpec((OUT_TM, D_MODEL), lambda i: (i, 0)),
            pl.BlockSpec((OUT_TM, POOL_WIDTH), lambda i: (i, 0)),
            pl.BlockSpec((OUT_TM, ATTN_WIDTH), lambda i: (i, 0)),
            pl.BlockSpec((OUT_TM, D_MODEL), lambda i: (i, COL_G_POOL // D_MODEL)),
            pl.BlockSpec((OUT_TM, D_MODEL), lambda i: (i, COL_G_ATTN // D_MODEL)),
            pl.BlockSpec((None, 1, D_MODEL), lambda i: (b_off + i // bpb, 0, 2)),
            resident((POOL_WIDTH, D_MODEL)),
            resident((ATTN_WIDTH, D_MODEL)),
            resident((D_MODEL, D_MODEL)),
        ],
        out_specs=pl.BlockSpec((OUT_TM, D_MODEL), lambda i: (i, 0)),
        out_shape=jax.ShapeDtypeStruct((m, D_MODEL), F32),
        scratch_shapes=[pltpu.VMEM((OUT_TM, D_MODEL), BF16)],
        compiler_params=pltpu.CompilerParams(
            dimension_semantics=("parallel",), vmem_limit_bytes=VMEM_LIMIT_BYTES),
        name="out_proj",
    )(x2, pool_y, attn_y, proj, proj, ada3, wpu_bf, wau_bf, wo_bf)


def kernel(x_prompt, x_sample, c_prompt, c_sample, w_ada, b_ada, norm_g, w_in, pool_w, pool_scale,
           q_norm_g, k_norm_g, rpb, w_pool_up, w_attn_up, w_o):
    assert w_ada.shape[0] == 1, "single-layer trunk"
    nb_prompt = x_prompt.shape[0]
    c_all = jnp.concatenate([c_prompt, c_sample], axis=0)
    ada = _ada(c_all, w_ada[0], b_ada[0])
    ada3 = ada.reshape(ada.shape[0], 1, 3 * D_MODEL)

    w_in_bf = w_in[0].astype(BF16)
    pool_w_bf = pool_w[0].astype(BF16)
    wpu_bf = w_pool_up[0].astype(BF16)
    wau_bf = w_attn_up[0].astype(BF16)
    wo_bf = w_o[0].astype(BF16)
    norm_g2 = norm_g[0].reshape(1, D_MODEL)
    pool_scale2 = pool_scale[0].reshape(1, POOL_WIDTH)
    qg_t = jnp.tile(q_norm_g[0] * (HEAD_DIM ** -0.5), N_HEADS).reshape(1, ATTN_WIDTH)
    kg_t = jnp.tile(k_norm_g[0], N_HEADS).reshape(1, ATTN_WIDTH)
    bias_tbl = _bias_table(rpb[0])

    def trunk(x, b_off):
        n_batch, seq, _ = x.shape
        x2 = x.reshape(n_batch * seq, D_MODEL)
        proj = _in_proj(x2, ada3, b_off, seq, norm_g2, w_in_bf, qg_t, kg_t)
        attn_y = _attn(proj, bias_tbl, n_batch, seq)
        pool_y = _pool(proj, pool_w_bf, pool_scale2, n_batch, seq)
        y = _out_proj(x2, pool_y, attn_y, proj, ada3, b_off, seq, wpu_bf, wau_bf, wo_bf)
        return y.reshape(n_batch, seq, D_MODEL)

    return (trunk(x_prompt, 0), trunk(x_sample, nb_prompt))
```

```python
import functools

import jax
import jax.numpy as jnp
from jax import lax
from jax.experimental import pallas as pl
from jax.experimental.pallas import tpu as pltpu

D_MODEL = 2048
GRID_W = 64
POOL_WIDTH = 1024
GROUP_W = 256
N_GROUPS = 4
HEAD_DIM = 64
N_HEADS = 16
ATTN_WIDTH = 1024
WIN_R = 8
WIN_C = 16
IN_WIDTH = 10240
NORM_EPS = 1e-6
MASK_BIAS = -1e30
LOG2_E = 1.4426950408889634

COL_POOL_U, COL_POOL_Z, COL_Q, COL_K, COL_V, COL_ATTN_Z, COL_G_POOL, COL_G_ATTN = (
    0, 1024, 2048, 3072, 4096, 5120, 6144, 8192)

VMEM_LIMIT_BYTES = 56 * 1024 * 1024

BF16 = jnp.bfloat16
F32 = jnp.float32


def _silu(x):
    return x * jax.nn.sigmoid(x)


ADA_TN = 512


def _ada_kernel(c_ref, w_ref, b_ref, o_ref):
    c = c_ref[...]
    a = _silu(c).astype(BF16)
    o_ref[...] = jnp.dot(a, w_ref[...].astype(BF16), preferred_element_type=F32) + b_ref[...]


def _ada(c_all, w_ada, b_ada):
    nb = c_all.shape[0]
    n = w_ada.shape[1]
    return pl.pallas_call(
        _ada_kernel,
        grid=(n // ADA_TN,),
        in_specs=[
            pl.BlockSpec((nb, D_MODEL), lambda j: (0, 0)),
            pl.BlockSpec((D_MODEL, ADA_TN), lambda j: (0, j)),
            pl.BlockSpec((1, ADA_TN), lambda j: (0, j)),
        ],
        out_specs=pl.BlockSpec((nb, ADA_TN), lambda j: (0, j)),
        out_shape=jax.ShapeDtypeStruct((nb, n), F32),
        compiler_params=pltpu.CompilerParams(dimension_semantics=("parallel",)),
        name="ada",
    )(c_all, w_ada, b_ada.reshape(1, n))


IN_TM = 1024
IN_TN = 1024
IN_ROW_CHUNK = 256
IN_NORM_ROWS = 16

TILE_POOL_U, TILE_POOL_Z, TILE_Q, TILE_K, TILE_V, TILE_ATTN_Z, TILE_GATES = 0, 1, 2, 3, 4, 5, 6


def _head_mean_square(acc):
    r = lax.broadcasted_iota(jnp.int32, (256, 256), 0) // HEAD_DIM
    c = lax.broadcasted_iota(jnp.int32, (256, 256), 1) // HEAD_DIM
    seg = (r == c).astype(BF16)
    sq = (acc * acc).astype(BF16)
    parts = [jnp.dot(sq[:, i * 256:(i + 1) * 256], seg, preferred_element_type=F32)
             for i in range(IN_TN // 256)]
    return jnp.concatenate(parts, axis=1) * (1.0 / HEAD_DIM)


def _in_proj_kernel(x_ref, shift_ref, scl_ref, g_ref, w_ref, hg_ref, o_ref, h_ref):
    j = pl.program_id(1)

    def normalize(rows0):
        gm = g_ref[...] * (1.0 + scl_ref[...])
        shift = shift_ref[...]
        for s in range(IN_ROW_CHUNK // IN_NORM_ROWS):
            rows = slice(rows0 + s * IN_NORM_ROWS, rows0 + (s + 1) * IN_NORM_ROWS)
            x = x_ref[rows, :]
            ms = jnp.mean(x * x, axis=-1, keepdims=True)
            h_ref[rows, :] = (x * lax.rsqrt(ms + NORM_EPS) * gm + shift).astype(BF16)

    def run(epilogue, with_norm=False):
        for c in range(IN_TM // IN_ROW_CHUNK):
            rows = slice(c * IN_ROW_CHUNK, (c + 1) * IN_ROW_CHUNK)
            if with_norm:
                normalize(c * IN_ROW_CHUNK)
            acc = jnp.dot(h_ref[rows, :], w_ref[...], preferred_element_type=F32)
            o_ref[rows, :] = epilogue(acc).astype(BF16)

    def head_norm(acc):
        return acc * lax.rsqrt(_head_mean_square(acc) + NORM_EPS) * hg_ref[...]

    @pl.when(j == TILE_POOL_U)
    def _():
        run(lambda a: a, with_norm=True)

    @pl.when(j == TILE_V)
    def _():
        run(lambda a: a)

    @pl.when((j == TILE_POOL_Z) | (j == TILE_ATTN_Z))
    def _():
        run(_silu)

    @pl.when((j == TILE_Q) | (j == TILE_K))
    def _():
        run(head_norm)

    @pl.when(j >= TILE_GATES)
    def _():
        run(jax.nn.sigmoid)


def _in_proj(x2, ada3, b_off, seq, norm_g, w_in_bf, head_gains):
    m = x2.shape[0]
    assert seq % IN_TM == 0 and IN_TN == ATTN_WIDTH == POOL_WIDTH
    bpb = seq // IN_TM
    return pl.pallas_call(
        _in_proj_kernel,
        grid=(m // IN_TM, IN_WIDTH // IN_TN),
        in_specs=[
            pl.BlockSpec((IN_TM, D_MODEL), lambda i, j: (i, 0)),
            pl.BlockSpec((None, 1, D_MODEL), lambda i, j: (b_off + i // bpb, 0, 0)),
            pl.BlockSpec((None, 1, D_MODEL), lambda i, j: (b_off + i // bpb, 0, 1)),
            pl.BlockSpec((1, D_MODEL), lambda i, j: (0, 0)),
            pl.BlockSpec((D_MODEL, IN_TN), lambda i, j: (0, j)),
            pl.BlockSpec((None, 1, IN_TN), lambda i, j: (jnp.where(j == TILE_K, 1, 0), 0, 0)),
        ],
        out_specs=pl.BlockSpec((IN_TM, IN_TN), lambda i, j: (i, j)),
        out_shape=jax.ShapeDtypeStruct((m, IN_WIDTH), BF16),
        scratch_shapes=[pltpu.VMEM((IN_TM, D_MODEL), BF16)],
        compiler_params=pltpu.CompilerParams(
            dimension_semantics=("parallel", "arbitrary"), vmem_limit_bytes=VMEM_LIMIT_BYTES),
        name="in_proj",
    )(x2, ada3, ada3, norm_g, w_in_bf, head_gains)


KEYS_PER_ROW_BLOCK = WIN_R * GRID_W
ATTN_ROWS_PER_STEP = 4


def _attn_kernel(q_ref, k_ref, v_ref, z_ref, tbl_ref, o_ref, *, rows):
    lane = lax.broadcasted_iota(jnp.int32, (GRID_W, 128), 1)
    first_head = lane < HEAD_DIM
    n_pairs = GROUP_W // 128

    def body(i, carry):
        chains = []
        for u in range(ATTN_ROWS_PER_STEP):
            r = i * ATTN_ROWS_PER_STEP + u
            rs = jnp.clip(r - WIN_R // 2, 0, rows - WIN_R)
            dr0 = rs - r + (WIN_R - 1)
            q0 = pl.multiple_of(r * GRID_W, GRID_W)
            k0 = pl.multiple_of(rs * GRID_W, GRID_W)
            for hp in range(n_pairs):
                chains.append((hp, dr0, q0, k0))

        scores = []
        for hp, dr0, q0, k0 in chains:
            lanes = slice(hp * 128, (hp + 1) * 128)
            q2 = q_ref[pl.ds(q0, GRID_W), lanes]
            zero = jnp.zeros_like(q2)
            q_stack = jnp.concatenate(
                [jnp.where(first_head, q2, zero), jnp.where(first_head, zero, q2)], axis=0)
            k2 = k_ref[pl.ds(k0, KEYS_PER_ROW_BLOCK), lanes]
            scores.append(lax.dot_general(q_stack, k2, (((1,), (1,)), ((), ())), preferred_element_type=F32))

        probs = []
        for (hp, dr0, q0, k0), s in zip(chains, scores):
            bias = jnp.concatenate(
                [jnp.concatenate([tbl_ref[2 * hp + e, dr0 + 2 * jj] for jj in range(WIN_R // 2)], axis=1)
                 for e in range(2)], axis=0)
            s = s + bias
            m = jnp.max(s, axis=-1, keepdims=True)
            p = jnp.exp2(s - m)
            l = jnp.sum(p, axis=-1, keepdims=True)
            probs.append((p.astype(BF16), l))

        for (hp, dr0, q0, k0), (p, l) in zip(chains, probs):
            lanes = slice(hp * 128, (hp + 1) * 128)
            v2 = v_ref[pl.ds(k0, KEYS_PER_ROW_BLOCK), lanes]
            pv = jnp.dot(p, v2, preferred_element_type=F32)
            pv = pv * (1.0 / l)
            o = jnp.where(first_head, pv[:GRID_W], pv[GRID_W:])
            o = o * z_ref[pl.ds(q0, GRID_W), lanes].astype(F32)
            o_ref[pl.ds(q0, GRID_W), lanes] = o.astype(BF16)
        return carry

    lax.fori_loop(0, rows // ATTN_ROWS_PER_STEP, body, 0)


def _attn(proj, bias_tbl, n_batch, seq):
    rows = seq // GRID_W
    assert rows % ATTN_ROWS_PER_STEP == 0 and rows >= WIN_R
    col = lambda off: (lambda b, g: (b, off // GROUP_W + g))
    blk = (seq, GROUP_W)
    return pl.pallas_call(
        functools.partial(_attn_kernel, rows=rows),
        grid=(n_batch, N_GROUPS),
        in_specs=[
            pl.BlockSpec(blk, col(COL_Q)),
            pl.BlockSpec(blk, col(COL_K)),
            pl.BlockSpec(blk, col(COL_V)),
            pl.BlockSpec(blk, col(COL_ATTN_Z)),
            pl.BlockSpec((N_HEADS // N_GROUPS, 2 * WIN_R - 2, GRID_W, 128), lambda b, g: (g, 0, 0, 0)),
        ],
        out_specs=pl.BlockSpec(blk, lambda b, g: (b, g)),
        out_shape=jax.ShapeDtypeStruct((n_batch * seq, ATTN_WIDTH), BF16),
        compiler_params=pltpu.CompilerParams(
            dimension_semantics=("parallel", "parallel"), vmem_limit_bytes=VMEM_LIMIT_BYTES),
        name="attn",
    )(proj, proj, proj, proj, bias_tbl)


def _bias_table(rpb):
    n_rel = 2 * WIN_C - 1
    c = jnp.arange(GRID_W)
    c_start = jnp.clip(c - WIN_C // 2, 0, GRID_W - WIN_C)
    in_win = (c[None, :] >= c_start[:, None]) & (c[None, :] < c_start[:, None] + WIN_C)
    period = GRID_W + n_rel
    padded = jnp.pad(rpb, ((0, 0), (0, 0), (0, GRID_W)))
    skew = jnp.tile(padded, (1, 1, GRID_W))[:, :, :GRID_W * (period - 1)]
    skew = skew.reshape(rpb.shape[0], rpb.shape[1], GRID_W, period - 1)
    rel_bias = skew[:, :, :, WIN_C - 1:WIN_C - 1 + GRID_W]
    t = jnp.where(in_win[None, None], rel_bias * LOG2_E, MASK_BIAS)
    return jnp.concatenate([t[:, :-1], t[:, 1:]], axis=-1).astype(F32)


POOL_CHUNK = 256
POOL_HALO = 16
POOL_SPAN = POOL_CHUNK + 2 * POOL_HALO


def _pool_kernel(u_ref, z_ref, w_ref, s_ref, o_ref, d_ref, *, seq):
    half = jnp.left_shift(1, pl.program_id(1))
    rel0 = (lax.broadcasted_iota(jnp.int32, (POOL_CHUNK, POOL_SPAN), 1)
            - lax.broadcasted_iota(jnp.int32, (POOL_CHUNK, POOL_SPAN), 0))
    bands = {}
    for ci in range(seq // POOL_CHUNK):
        t0 = ci * POOL_CHUNK
        start = min(max(t0 - POOL_HALO, 0), seq - POOL_SPAN)
        if start - t0 not in bands:
            rel = rel0 + (start - t0)
            bands[start - t0] = ((rel >= -half) & (rel < half)).astype(BF16)
        window_sum = jnp.dot(bands[start - t0], u_ref[start:start + POOL_SPAN, :], preferred_element_type=F32)
        t = t0 + lax.broadcasted_iota(jnp.int32, (POOL_CHUNK, 1), 0)
        cnt = jnp.minimum(t + half - 1, seq - 1) - jnp.maximum(t - half, 0) + 1
        d = window_sum / cnt.astype(F32) - u_ref[t0:t0 + POOL_CHUNK, :].astype(F32)
        d_ref[t0:t0 + POOL_CHUNK, :] = d.astype(BF16)
    w_mat = w_ref[...]
    scale = s_ref[...]
    for ci in range(seq // POOL_CHUNK):
        t0 = ci * POOL_CHUNK
        y = jnp.dot(d_ref[t0:t0 + POOL_CHUNK, :], w_mat, preferred_element_type=F32)
        y = y * scale * z_ref[t0:t0 + POOL_CHUNK, :].astype(F32)
        o_ref[t0:t0 + POOL_CHUNK, :] = y.astype(BF16)


def _pool(proj, pool_w_bf, pool_scale2, n_batch, seq):
    assert seq % POOL_CHUNK == 0 and seq >= POOL_SPAN and N_GROUPS == 4
    blk = (seq, GROUP_W)
    return pl.pallas_call(
        functools.partial(_pool_kernel, seq=seq),
        grid=(n_batch, N_GROUPS),
        in_specs=[
            pl.BlockSpec(blk, lambda b, g: (b, COL_POOL_U // GROUP_W + g)),
            pl.BlockSpec(blk, lambda b, g: (b, COL_POOL_Z // GROUP_W + g)),
            pl.BlockSpec((None, GROUP_W, GROUP_W), lambda b, g: (g, 0, 0)),
            pl.BlockSpec((1, GROUP_W), lambda b, g: (0, g)),
        ],
        out_specs=pl.BlockSpec(blk, lambda b, g: (b, g)),
        out_shape=jax.ShapeDtypeStruct((n_batch * seq, POOL_WIDTH), BF16),
        scratch_shapes=[pltpu.VMEM((seq, GROUP_W), BF16)],
        compiler_params=pltpu.CompilerParams(
            dimension_semantics=("parallel", "parallel"), vmem_limit_bytes=VMEM_LIMIT_BYTES),
        name="pool",
    )(proj, proj, pool_w_bf, pool_scale2)


OUT_TM = 512
OUT_CHUNK = 512


def _out_proj_kernel(x_ref, py_ref, ay_ref, gp_ref, ga_ref, gate_ref, wpu_ref, wau_ref, wo_ref, o_ref, m_ref):
    py = py_ref[...]
    ay = ay_ref[...]
    for ci in range(D_MODEL // OUT_CHUNK):
        cs = slice(ci * OUT_CHUNK, (ci + 1) * OUT_CHUNK)
        pooled = jnp.dot(py, wpu_ref[:, cs], preferred_element_type=F32)
        attended = jnp.dot(ay, wau_ref[:, cs], preferred_element_type=F32)
        merged = gp_ref[:, cs].astype(F32) * pooled + ga_ref[:, cs].astype(F32) * attended
        m_ref[:, cs] = merged.astype(BF16)
    merged = m_ref[...]
    for ci in range(D_MODEL // OUT_CHUNK):
        cs = slice(ci * OUT_CHUNK, (ci + 1) * OUT_CHUNK)
        y = jnp.dot(merged, wo_ref[:, cs], preferred_element_type=F32)
        o_ref[:, cs] = x_ref[:, cs] + gate_ref[:, cs] * y


def _out_proj(x2, pool_y, attn_y, proj, ada3, b_off, seq, wpu_bf, wau_bf, wo_bf):
    m = x2.shape[0]
    assert seq % OUT_TM == 0
    bpb = seq // OUT_TM
    resident = lambda shape: pl.BlockSpec(shape, lambda i: (0, 0), pipeline_mode=pl.Buffered(1))
    return pl.pallas_call(
        _out_proj_kernel,
        grid=(m // OUT_TM,),
        in_specs=[
            pl.BlockSpec((OUT_TM, D_MODEL), lambda i: (i, 0)),
            pl.BlockSpec((OUT_TM, POOL_WIDTH), lambda i: (i, 0)),
            pl.BlockSpec((OUT_TM, ATTN_WIDTH), lambda i: (i, 0)),
            pl.BlockSpec((OUT_TM, D_MODEL), lambda i: (i, COL_G_POOL // D_MODEL)),
            pl.BlockSpec((OUT_TM, D_MODEL), lambda i: (i, COL_G_ATTN // D_MODEL)),
            pl.BlockSpec((None, 1, D_MODEL), lambda i: (b_off + i // bpb, 0, 2)),
            resident((POOL_WIDTH, D_MODEL)),
            resident((ATTN_WIDTH, D_MODEL)),
            resident((D_MODEL, D_MODEL)),
        ],
        out_specs=pl.BlockSpec((OUT_TM, D_MODEL), lambda i: (i, 0)),
        out_shape=jax.ShapeDtypeStruct((m, D_MODEL), F32),
        scratch_shapes=[pltpu.VMEM((OUT_TM, D_MODEL), BF16)],
        compiler_params=pltpu.CompilerParams(
            dimension_semantics=("parallel",), vmem_limit_bytes=VMEM_LIMIT_BYTES),
        name="out_proj",
    )(x2, pool_y, attn_y, proj, proj, ada3, wpu_bf, wau_bf, wo_bf)


def kernel(x_prompt, x_sample, c_prompt, c_sample, w_ada, b_ada, norm_g, w_in, pool_w, pool_scale,
           q_norm_g, k_norm_g, rpb, w_pool_up, w_attn_up, w_o):
    assert w_ada.shape[0] == 1, "single-layer trunk"
    nb_prompt = x_prompt.shape[0]
    c_all = jnp.concatenate([c_prompt, c_sample], axis=0)
    ada = _ada(c_all, w_ada[0], b_ada[0])
    ada3 = ada.reshape(ada.shape[0], 1, 3 * D_MODEL)

    w_in_bf = w_in[0].astype(BF16)
    pool_w_bf = pool_w[0].astype(BF16)
    wpu_bf = w_pool_up[0].astype(BF16)
    wau_bf = w_attn_up[0].astype(BF16)
    wo_bf = w_o[0].astype(BF16)
    norm_g2 = norm_g[0].reshape(1, D_MODEL)
    pool_scale2 = pool_scale[0].reshape(1, POOL_WIDTH)
    qg_t = jnp.tile(q_norm_g[0] * (HEAD_DIM ** -0.5 * LOG2_E), N_HEADS)
    kg_t = jnp.tile(k_norm_g[0], N_HEADS)
    head_gains = jnp.stack([qg_t, kg_t]).reshape(2, 1, ATTN_WIDTH)
    bias_tbl = _bias_table(rpb[0])

    def trunk(x, b_off):
        n_batch, seq, _ = x.shape
        x2 = x.reshape(n_batch * seq, D_MODEL)
        proj = _in_proj(x2, ada3, b_off, seq, norm_g2, w_in_bf, head_gains)
        attn_y = _attn(proj, bias_tbl, n_batch, seq)
        pool_y = _pool(proj, pool_w_bf, pool_scale2, n_batch, seq)
        y = _out_proj(x2, pool_y, attn_y, proj, ada3, b_off, seq, wpu_bf, wau_bf, wo_bf)
        return y.reshape(n_batch, seq, D_MODEL)

    return (trunk(x_prompt, 0), trunk(x_sample, nb_prompt))
```

```python
import functools

import jax
import jax.numpy as jnp
from jax import lax
from jax.experimental import pallas as pl
from jax.experimental.pallas import tpu as pltpu

D_MODEL = 2048
GRID_W = 64
POOL_WIDTH = 1024
GROUP_W = 256
N_GROUPS = 4
HEAD_DIM = 64
N_HEADS = 16
ATTN_WIDTH = 1024
WIN_R = 8
WIN_C = 16
IN_WIDTH = 10240
NORM_EPS = 1e-6
MASK_BIAS = -1e30
LOG2_E = 1.4426950408889634

COL_POOL_U, COL_POOL_Z, COL_Q, COL_K, COL_V, COL_ATTN_Z, COL_G_POOL, COL_G_ATTN = (
    0, 1024, 2048, 3072, 4096, 5120, 6144, 8192)

VMEM_LIMIT_BYTES = 56 * 1024 * 1024

BF16 = jnp.bfloat16
F32 = jnp.float32


def _silu(x):
    return x * jax.nn.sigmoid(x)


ADA_TN = 512


def _ada_kernel(c_ref, w_ref, b_ref, o_ref):
    c = c_ref[...]
    a = _silu(c).astype(BF16)
    o_ref[...] = jnp.dot(a, w_ref[...].astype(BF16), preferred_element_type=F32) + b_ref[...]


def _ada(c_all, w_ada, b_ada):
    nb = c_all.shape[0]
    n = w_ada.shape[1]
    return pl.pallas_call(
        _ada_kernel,
        grid=(n // ADA_TN,),
        in_specs=[
            pl.BlockSpec((nb, D_MODEL), lambda j: (0, 0)),
            pl.BlockSpec((D_MODEL, ADA_TN), lambda j: (0, j)),
            pl.BlockSpec((1, ADA_TN), lambda j: (0, j)),
        ],
        out_specs=pl.BlockSpec((nb, ADA_TN), lambda j: (0, j)),
        out_shape=jax.ShapeDtypeStruct((nb, n), F32),
        compiler_params=pltpu.CompilerParams(dimension_semantics=("parallel",)),
        name="ada",
    )(c_all, w_ada, b_ada.reshape(1, n))


IN_TM = 1024
IN_TN = 2048
IN_GROUP_W = 1024
IN_ROW_CHUNK = 256
IN_NORM_ROWS = 16

TILE_POOL, TILE_QK, TILE_V_ATTN_Z, TILE_GATES = 0, 1, 2, 3


def _head_mean_square(acc):
    r = lax.broadcasted_iota(jnp.int32, (256, 256), 0) // HEAD_DIM
    c = lax.broadcasted_iota(jnp.int32, (256, 256), 1) // HEAD_DIM
    seg = (r == c).astype(BF16)
    sq = (acc * acc).astype(BF16)
    parts = [jnp.dot(sq[:, i * 256:(i + 1) * 256], seg, preferred_element_type=F32)
             for i in range(IN_GROUP_W // 256)]
    return jnp.concatenate(parts, axis=1) * (1.0 / HEAD_DIM)


def _in_proj_kernel(x_ref, shift_ref, scl_ref, g_ref, w_ref, hg_ref, o_ref, h_ref):
    j = pl.program_id(1)

    def normalize(rows0):
        gm = g_ref[...] * (1.0 + scl_ref[...])
        shift = shift_ref[...]
        for s in range(IN_ROW_CHUNK // IN_NORM_ROWS):
            rows = slice(rows0 + s * IN_NORM_ROWS, rows0 + (s + 1) * IN_NORM_ROWS)
            x = x_ref[rows, :]
            ms = jnp.mean(x * x, axis=-1, keepdims=True)
            h_ref[rows, :] = (x * lax.rsqrt(ms + NORM_EPS) * gm + shift).astype(BF16)

    def run(epilogues, with_norm=False):
        for c in range(IN_TM // IN_ROW_CHUNK):
            rows = slice(c * IN_ROW_CHUNK, (c + 1) * IN_ROW_CHUNK)
            if with_norm:
                normalize(c * IN_ROW_CHUNK)
            for half, epilogue in enumerate(epilogues):
                cols = slice(half * IN_GROUP_W, (half + 1) * IN_GROUP_W)
                acc = jnp.dot(h_ref[rows, :], w_ref[:, cols], preferred_element_type=F32)
                o_ref[rows, cols] = epilogue(acc).astype(BF16)

    def head_norm(which):
        return lambda acc: acc * lax.rsqrt(_head_mean_square(acc) + NORM_EPS) * hg_ref[which]

    plain = lambda acc: acc

    @pl.when(j == TILE_POOL)
    def _():
        run((plain, _silu), with_norm=True)

    @pl.when(j == TILE_QK)
    def _():
        run((head_norm(0), head_norm(1)))

    @pl.when(j == TILE_V_ATTN_Z)
    def _():
        run((plain, _silu))

    @pl.when(j >= TILE_GATES)
    def _():
        run((jax.nn.sigmoid, jax.nn.sigmoid))


def _in_proj(x2, ada3, b_off, seq, norm_g, w_in_bf, head_gains):
    m = x2.shape[0]
    assert seq % IN_TM == 0 and IN_GROUP_W == ATTN_WIDTH == POOL_WIDTH
    bpb = seq // IN_TM
    return pl.pallas_call(
        _in_proj_kernel,
        grid=(m // IN_TM, IN_WIDTH // IN_TN),
        in_specs=[
            pl.BlockSpec((IN_TM, D_MODEL), lambda i, j: (i, 0)),
            pl.BlockSpec((None, 1, D_MODEL), lambda i, j: (b_off + i // bpb, 0, 0)),
            pl.BlockSpec((None, 1, D_MODEL), lambda i, j: (b_off + i // bpb, 0, 1)),
            pl.BlockSpec((1, D_MODEL), lambda i, j: (0, 0)),
            pl.BlockSpec((D_MODEL, IN_TN), lambda i, j: (0, j)),
            pl.BlockSpec((2, 1, IN_GROUP_W), lambda i, j: (0, 0, 0)),
        ],
        out_specs=pl.BlockSpec((IN_TM, IN_TN), lambda i, j: (i, j)),
        out_shape=jax.ShapeDtypeStruct((m, IN_WIDTH), BF16),
        scratch_shapes=[pltpu.VMEM((IN_TM, D_MODEL), BF16)],
        compiler_params=pltpu.CompilerParams(
            dimension_semantics=("parallel", "arbitrary"), vmem_limit_bytes=VMEM_LIMIT_BYTES),
        name="in_proj",
    )(x2, ada3, ada3, norm_g, w_in_bf, head_gains)


ATTN_ROWS_PER_STEP = 8
HEADS_PER_GROUP = N_HEADS // N_GROUPS
SPAN_C = 32
ATTN_BLOCKS = ((0, 24, 0), (24, 16, 16), (40, 24, 32))
SPAN_KEYS = WIN_R * SPAN_C


def _attn_kernel(q_ref, k_ref, v_ref, z_ref, tbl_ref, o_ref, *, rows):
    lane_head = lax.broadcasted_iota(jnp.int32, (1, GROUP_W), 1) // HEAD_DIM
    head_lanes = [lane_head == h for h in range(HEADS_PER_GROUP)]

    def span_rows(ref, k0, sc):
        return jnp.concatenate(
            [ref[pl.ds(pl.multiple_of(k0 + j * GRID_W + sc, 16), SPAN_C), :] for j in range(WIN_R)], axis=0)

    def body(i, carry):
        chains = []
        for u in range(ATTN_ROWS_PER_STEP):
            r = i * ATTN_ROWS_PER_STEP + u
            rs = jnp.clip(r - WIN_R // 2, 0, rows - WIN_R)
            dr0 = rs - r + (WIN_R - 1)
            q0 = pl.multiple_of(r * GRID_W, GRID_W)
            k0 = pl.multiple_of(rs * GRID_W, GRID_W)
            qf = q_ref[pl.ds(q0, GRID_W), :].astype(F32)
            tbl_row = 0
            for qc0, nq, sc in ATTN_BLOCKS:
                chains.append((dr0, k0, qf[qc0:qc0 + nq], nq, sc, tbl_row))
                tbl_row += HEADS_PER_GROUP * nq

        scores = []
        for dr0, k0, qb, nq, sc, tbl_row in chains:
            lhs = jnp.concatenate([jnp.where(hl, qb, 0.0) for hl in head_lanes], axis=0).astype(BF16)
            scores.append(lax.dot_general(lhs, span_rows(k_ref, k0, sc), (((1,), (1,)), ((), ())),
                                          preferred_element_type=F32))

        probs = []
        for (dr0, k0, qb, nq, sc, tbl_row), s in zip(chains, scores):
            s = s + tbl_ref[dr0, tbl_row:tbl_row + HEADS_PER_GROUP * nq, :]
            m = jnp.max(s, axis=-1, keepdims=True)
            p = jnp.exp2(s - m)
            l = jnp.sum(p, axis=-1, keepdims=True)
            probs.append((p.astype(BF16), l))

        blocks = []
        for (dr0, k0, qb, nq, sc, tbl_row), (p, l) in zip(chains, probs):
            pv = jnp.dot(p, span_rows(v_ref, k0, sc), preferred_element_type=F32) * (1.0 / l)
            ob = pv[:nq]
            for h in range(1, HEADS_PER_GROUP):
                ob = jnp.where(head_lanes[h], pv[h * nq:(h + 1) * nq], ob)
            blocks.append(ob)

        for u in range(ATTN_ROWS_PER_STEP):
            q0 = pl.multiple_of((i * ATTN_ROWS_PER_STEP + u) * GRID_W, GRID_W)
            o = jnp.concatenate(blocks[u * len(ATTN_BLOCKS):(u + 1) * len(ATTN_BLOCKS)], axis=0)
            o = o * z_ref[pl.ds(q0, GRID_W), :].astype(F32)
            o_ref[pl.ds(q0, GRID_W), :] = o.astype(BF16)
        return carry

    lax.fori_loop(0, rows // ATTN_ROWS_PER_STEP, body, 0)


def _attn(proj, bias_tbl, n_batch, seq):
    rows = seq // GRID_W
    assert rows % ATTN_ROWS_PER_STEP == 0 and rows >= WIN_R
    col = lambda off: (lambda b, g: (b, off // GROUP_W + g))
    blk = (seq, GROUP_W)
    return pl.pallas_call(
        functools.partial(_attn_kernel, rows=rows),
        grid=(n_batch, N_GROUPS),
        in_specs=[
            pl.BlockSpec(blk, col(COL_Q)),
            pl.BlockSpec(blk, col(COL_K)),
            pl.BlockSpec(blk, col(COL_V)),
            pl.BlockSpec(blk, col(COL_ATTN_Z)),
            pl.BlockSpec((None, WIN_R, HEADS_PER_GROUP * GRID_W, SPAN_KEYS), lambda b, g: (g, 0, 0, 0)),
        ],
        out_specs=pl.BlockSpec(blk, lambda b, g: (b, g)),
        out_shape=jax.ShapeDtypeStruct((n_batch * seq, ATTN_WIDTH), BF16),
        compiler_params=pltpu.CompilerParams(
            dimension_semantics=("parallel", "parallel"), vmem_limit_bytes=VMEM_LIMIT_BYTES),
        name="attn",
    )(proj, proj, proj, proj, bias_tbl)


def _bias_table(rpb):
    n_rel = 2 * WIN_C - 1
    c = jnp.arange(GRID_W)
    c_start = jnp.clip(c - WIN_C // 2, 0, GRID_W - WIN_C)
    in_win = (c[None, :] >= c_start[:, None]) & (c[None, :] < c_start[:, None] + WIN_C)
    period = GRID_W + n_rel
    padded = jnp.pad(rpb, ((0, 0), (0, 0), (0, GRID_W)))
    skew = jnp.tile(padded, (1, 1, GRID_W))[:, :, :GRID_W * (period - 1)]
    skew = skew.reshape(rpb.shape[0], rpb.shape[1], GRID_W, period - 1)
    rel_bias = skew[:, :, :, WIN_C - 1:WIN_C - 1 + GRID_W]
    t = jnp.where(in_win[None, None], rel_bias * LOG2_E, MASK_BIAS).astype(F32)
    per_block = []
    for qc0, nq, sc in ATTN_BLOCKS:
        sub = t[:, :, qc0:qc0 + nq, sc:sc + SPAN_C]
        win = jnp.stack([sub[:, d:d + WIN_R] for d in range(WIN_R)], axis=1)
        win = win.transpose(0, 1, 3, 2, 4).reshape(N_GROUPS, HEADS_PER_GROUP, WIN_R, nq, SPAN_KEYS)
        per_block.append(win.transpose(0, 2, 1, 3, 4).reshape(N_GROUPS, WIN_R, HEADS_PER_GROUP * nq, SPAN_KEYS))
    return jnp.concatenate(per_block, axis=2)


POOL_CHUNK = 256
POOL_HALO = 16
POOL_SPAN = POOL_CHUNK + 2 * POOL_HALO


def _pool_kernel(u_ref, z_ref, w_ref, s_ref, o_ref, d_ref, *, seq):
    half = jnp.left_shift(1, pl.program_id(1))
    rel0 = (lax.broadcasted_iota(jnp.int32, (POOL_CHUNK, POOL_SPAN), 1)
            - lax.broadcasted_iota(jnp.int32, (POOL_CHUNK, POOL_SPAN), 0))
    bands = {}
    for ci in range(seq // POOL_CHUNK):
        t0 = ci * POOL_CHUNK
        start = min(max(t0 - POOL_HALO, 0), seq - POOL_SPAN)
        if start - t0 not in bands:
            rel = rel0 + (start - t0)
            bands[start - t0] = ((rel >= -half) & (rel < half)).astype(BF16)
        window_sum = jnp.dot(bands[start - t0], u_ref[start:start + POOL_SPAN, :], preferred_element_type=F32)
        t = t0 + lax.broadcasted_iota(jnp.int32, (POOL_CHUNK, 1), 0)
        cnt = jnp.minimum(t + half - 1, seq - 1) - jnp.maximum(t - half, 0) + 1
        d = window_sum / cnt.astype(F32) - u_ref[t0:t0 + POOL_CHUNK, :].astype(F32)
        d_ref[t0:t0 + POOL_CHUNK, :] = d.astype(BF16)
    w_mat = w_ref[...]
    scale = s_ref[...]
    for ci in range(seq // POOL_CHUNK):
        t0 = ci * POOL_CHUNK
        y = jnp.dot(d_ref[t0:t0 + POOL_CHUNK, :], w_mat, preferred_element_type=F32)
        y = y * scale * z_ref[t0:t0 + POOL_CHUNK, :].astype(F32)
        o_ref[t0:t0 + POOL_CHUNK, :] = y.astype(BF16)


def _pool(proj, pool_w_bf, pool_scale2, n_batch, seq):
    assert seq % POOL_CHUNK == 0 and seq >= POOL_SPAN and N_GROUPS == 4
    blk = (seq, GROUP_W)
    return pl.pallas_call(
        functools.partial(_pool_kernel, seq=seq),
        grid=(n_batch, N_GROUPS),
        in_specs=[
            pl.BlockSpec(blk, lambda b, g: (b, COL_POOL_U // GROUP_W + g)),
            pl.BlockSpec(blk, lambda b, g: (b, COL_POOL_Z // GROUP_W + g)),
            pl.BlockSpec((None, GROUP_W, GROUP_W), lambda b, g: (g, 0, 0)),
            pl.BlockSpec((1, GROUP_W), lambda b, g: (0, g)),
        ],
        out_specs=pl.BlockSpec(blk, lambda b, g: (b, g)),
        out_shape=jax.ShapeDtypeStruct((n_batch * seq, POOL_WIDTH), BF16),
        scratch_shapes=[pltpu.VMEM((seq, GROUP_W), BF16)],
        compiler_params=pltpu.CompilerParams(
            dimension_semantics=("parallel", "parallel"), vmem_limit_bytes=VMEM_LIMIT_BYTES),
        name="pool",
    )(proj, proj, pool_w_bf, pool_scale2)


OUT_TM = 512
OUT_CHUNK = 512


def _out_proj_kernel(x_ref, py_ref, ay_ref, gp_ref, ga_ref, gate_ref, wpu_ref, wau_ref, wo_ref, o_ref, m_ref):
    py = py_ref[...]
    ay = ay_ref[...]
    for ci in range(D_MODEL // OUT_CHUNK):
        cs = slice(ci * OUT_CHUNK, (ci + 1) * OUT_CHUNK)
        pooled = jnp.dot(py, wpu_ref[:, cs], preferred_element_type=F32)
        attended = jnp.dot(ay, wau_ref[:, cs], preferred_element_type=F32)
        merged = gp_ref[:, cs].astype(F32) * pooled + ga_ref[:, cs].astype(F32) * attended
        m_ref[:, cs] = merged.astype(BF16)
    merged = m_ref[...]
    for ci in range(D_MODEL // OUT_CHUNK):
        cs = slice(ci * OUT_CHUNK, (ci + 1) * OUT_CHUNK)
        y = jnp.dot(merged, wo_ref[:, cs], preferred_element_type=F32)
        o_ref[:, cs] = x_ref[:, cs] + gate_ref[:, cs] * y


def _out_proj(x2, pool_y, attn_y, proj, ada3, b_off, seq, wpu_bf, wau_bf, wo_bf):
    m = x2.shape[0]
    assert seq % OUT_TM == 0
    bpb = seq // OUT_TM
    resident = lambda shape: pl.BlockSpec(shape, lambda i: (0, 0), pipeline_mode=pl.Buffered(1))
    return pl.pallas_call(
        _out_proj_kernel,
        grid=(m // OUT_TM,),
        in_specs=[
            pl.BlockSpec((OUT_TM, D_MODEL), lambda i: (i, 0)),
            pl.BlockSpec((OUT_TM, POOL_WIDTH), lambda i: (i, 0)),
            pl.BlockSpec((OUT_TM, ATTN_WIDTH), lambda i: (i, 0)),
            pl.BlockSpec((OUT_TM, D_MODEL), lambda i: (i, COL_G_POOL // D_MODEL)),
            pl.BlockSpec((OUT_TM, D_MODEL), lambda i: (i, COL_G_ATTN // D_MODEL)),
            pl.BlockSpec((None, 1, D_MODEL), lambda i: (b_off + i // bpb, 0, 2)),
            resident((POOL_WIDTH, D_MODEL)),
            resident((ATTN_WIDTH, D_MODEL)),
            resident((D_MODEL, D_MODEL)),
        ],
        out_specs=pl.BlockSpec((OUT_TM, D_MODEL), lambda i: (i, 0)),
        out_shape=jax.ShapeDtypeStruct((m, D_MODEL), F32),
        scratch_shapes=[pltpu.VMEM((OUT_TM, D_MODEL), BF16)],
        compiler_params=pltpu.CompilerParams(
            dimension_semantics=("parallel",), vmem_limit_bytes=VMEM_LIMIT_BYTES),
        name="out_proj",
    )(x2, pool_y, attn_y, proj, proj, ada3, wpu_bf, wau_bf, wo_bf)


def kernel(x_prompt, x_sample, c_prompt, c_sample, w_ada, b_ada, norm_g, w_in, pool_w, pool_scale,
           q_norm_g, k_norm_g, rpb, w_pool_up, w_attn_up, w_o):
    assert w_ada.shape[0] == 1, "single-layer trunk"
    nb_prompt = x_prompt.shape[0]
    c_all = jnp.concatenate([c_prompt, c_sample], axis=0)
    ada = _ada(c_all, w_ada[0], b_ada[0])
    ada3 = ada.reshape(ada.shape[0], 1, 3 * D_MODEL)

    w_in_bf = w_in[0].astype(BF16)
    pool_w_bf = pool_w[0].astype(BF16)
    wpu_bf = w_pool_up[0].astype(BF16)
    wau_bf = w_attn_up[0].astype(BF16)
    wo_bf = w_o[0].astype(BF16)
    norm_g2 = norm_g[0].reshape(1, D_MODEL)
    pool_scale2 = pool_scale[0].reshape(1, POOL_WIDTH)
    qg_t = jnp.tile(q_norm_g[0] * (HEAD_DIM ** -0.5 * LOG2_E), N_HEADS)
    kg_t = jnp.tile(k_norm_g[0], N_HEADS)
    head_gains = jnp.stack([qg_t, kg_t]).reshape(2, 1, ATTN_WIDTH)
    bias_tbl = _bias_table(rpb[0])

    def trunk(x, b_off):
        n_batch, seq, _ = x.shape
        x2 = x.reshape(n_batch * seq, D_MODEL)
        proj = _in_proj(x2, ada3, b_off, seq, norm_g2, w_in_bf, head_gains)
        attn_y = _attn(proj, bias_tbl, n_batch, seq)
        pool_y = _pool(proj, pool_w_bf, pool_scale2, n_batch, seq)
        y = _out_proj(x2, pool_y, attn_y, proj, ada3, b_off, seq, wpu_bf, wau_bf, wo_bf)
        return y.reshape(n_batch, seq, D_MODEL)

    return (trunk(x_prompt, 0), trunk(x_sample, nb_prompt))
```

```python
import functools

import jax
import jax.numpy as jnp
import numpy as np
from jax import lax
from jax.experimental import pallas as pl
from jax.experimental.pallas import tpu as pltpu

D_MODEL = 2048
GRID_W = 64
POOL_WIDTH = 1024
GROUP_W = 256
N_GROUPS = 4
HEAD_DIM = 64
N_HEADS = 16
ATTN_WIDTH = 1024
WIN_R = 8
WIN_C = 16
IN_WIDTH = 10240
NORM_EPS = 1e-6
MASK_BIAS = -1e30
LOG2_E = 1.4426950408889634

COL_POOL_U, COL_POOL_Z, COL_Q, COL_K, COL_V, COL_ATTN_Z, COL_G_POOL, COL_G_ATTN = (
    0, 1024, 2048, 3072, 4096, 5120, 6144, 8192)
N_SLABS = IN_WIDTH // GROUP_W

VMEM_LIMIT_BYTES = 56 * 1024 * 1024

BF16 = jnp.bfloat16
F32 = jnp.float32


def _silu(x):
    return x * jax.nn.sigmoid(x)


ADA_TN = 512


def _ada_kernel(c_ref, w_ref, b_ref, o_ref):
    c = c_ref[...]
    a = _silu(c).astype(BF16)
    o_ref[...] = jnp.dot(a, w_ref[...].astype(BF16), preferred_element_type=F32) + b_ref[...]


def _ada(c_all, w_ada, b_ada):
    nb = c_all.shape[0]
    n = w_ada.shape[1]
    return pl.pallas_call(
        _ada_kernel,
        grid=(n // ADA_TN,),
        in_specs=[
            pl.BlockSpec((nb, D_MODEL), lambda j: (0, 0)),
            pl.BlockSpec((D_MODEL, ADA_TN), lambda j: (0, j)),
            pl.BlockSpec((1, ADA_TN), lambda j: (0, j)),
        ],
        out_specs=pl.BlockSpec((nb, ADA_TN), lambda j: (0, j)),
        out_shape=jax.ShapeDtypeStruct((nb, n), F32),
        compiler_params=pltpu.CompilerParams(dimension_semantics=("parallel",)),
        name="ada",
    )(c_all, w_ada, b_ada.reshape(1, n))


IN_TM = 1024
IN_TN = 2048
IN_GROUP_W = 1024
IN_ROW_CHUNK = 256
IN_NORM_ROWS = 16

TILE_POOL, TILE_QK, TILE_V_ATTN_Z, TILE_GATES = 0, 1, 2, 3


def _head_mean_square(acc):
    r = lax.broadcasted_iota(jnp.int32, (256, 256), 0) // HEAD_DIM
    c = lax.broadcasted_iota(jnp.int32, (256, 256), 1) // HEAD_DIM
    seg = (r == c).astype(BF16)
    sq = (acc * acc).astype(BF16)
    parts = [jnp.dot(sq[:, i * 256:(i + 1) * 256], seg, preferred_element_type=F32)
             for i in range(IN_GROUP_W // 256)]
    return jnp.concatenate(parts, axis=1) * (1.0 / HEAD_DIM)


def _in_proj_kernel(x_ref, shift_ref, scl_ref, g_ref, w_ref, hg_ref, o_ref, h_ref):
    j = pl.program_id(1)

    def normalize(rows0):
        gm = g_ref[...] * (1.0 + scl_ref[...])
        shift = shift_ref[...]
        for s in range(IN_ROW_CHUNK // IN_NORM_ROWS):
            rows = slice(rows0 + s * IN_NORM_ROWS, rows0 + (s + 1) * IN_NORM_ROWS)
            x = x_ref[rows, :]
            ms = jnp.mean(x * x, axis=-1, keepdims=True)
            h_ref[rows, :] = (x * lax.rsqrt(ms + NORM_EPS) * gm + shift).astype(BF16)

    def run(epilogues, with_norm=False):
        for c in range(IN_TM // IN_ROW_CHUNK):
            rows = slice(c * IN_ROW_CHUNK, (c + 1) * IN_ROW_CHUNK)
            if with_norm:
                normalize(c * IN_ROW_CHUNK)
            for half, epilogue in enumerate(epilogues):
                cols = slice(half * IN_GROUP_W, (half + 1) * IN_GROUP_W)
                acc = jnp.dot(h_ref[rows, :], w_ref[:, cols], preferred_element_type=F32)
                out = epilogue(acc).astype(BF16)
                for sl in range(IN_GROUP_W // GROUP_W):
                    o_ref[half * (IN_GROUP_W // GROUP_W) + sl, rows, :] = out[:, sl * GROUP_W:(sl + 1) * GROUP_W]

    def head_norm(which):
        return lambda acc: acc * lax.rsqrt(_head_mean_square(acc) + NORM_EPS) * hg_ref[which]

    plain = lambda acc: acc

    @pl.when(j == TILE_POOL)
    def _():
        run((plain, _silu), with_norm=True)

    @pl.when(j == TILE_QK)
    def _():
        run((head_norm(0), head_norm(1)))

    @pl.when(j == TILE_V_ATTN_Z)
    def _():
        run((plain, _silu))

    @pl.when(j >= TILE_GATES)
    def _():
        run((jax.nn.sigmoid, jax.nn.sigmoid))


def _in_proj(x2, ada3, b_off, seq, norm_g, w_in_bf, head_gains):
    m = x2.shape[0]
    assert seq % IN_TM == 0 and IN_GROUP_W == ATTN_WIDTH == POOL_WIDTH
    bpb = seq // IN_TM
    return pl.pallas_call(
        _in_proj_kernel,
        grid=(m // IN_TM, IN_WIDTH // IN_TN),
        in_specs=[
            pl.BlockSpec((IN_TM, D_MODEL), lambda i, j: (i, 0)),
            pl.BlockSpec((None, 1, D_MODEL), lambda i, j: (b_off + i // bpb, 0, 0)),
            pl.BlockSpec((None, 1, D_MODEL), lambda i, j: (b_off + i // bpb, 0, 1)),
            pl.BlockSpec((1, D_MODEL), lambda i, j: (0, 0)),
            pl.BlockSpec((D_MODEL, IN_TN), lambda i, j: (0, j)),
            pl.BlockSpec((2, 1, IN_GROUP_W), lambda i, j: (0, 0, 0)),
        ],
        out_specs=pl.BlockSpec((IN_TN // GROUP_W, IN_TM, GROUP_W), lambda i, j: (j, i, 0)),
        out_shape=jax.ShapeDtypeStruct((N_SLABS, m, GROUP_W), BF16),
        scratch_shapes=[pltpu.VMEM((IN_TM, D_MODEL), BF16)],
        compiler_params=pltpu.CompilerParams(
            dimension_semantics=("parallel", "arbitrary"), vmem_limit_bytes=VMEM_LIMIT_BYTES),
        name="in_proj",
    )(x2, ada3, ada3, norm_g, w_in_bf, head_gains)


ATTN_ROWS_PER_STEP = 16
HEADS_PER_GROUP = N_HEADS // N_GROUPS
SPAN_C = 32
ATTN_BLOCKS = ((0, 24, 0), (24, 16, 16), (40, 24, 32))
SPAN_KEYS = WIN_R * SPAN_C
BLOCK_Q_MAX = max(nq for _, nq, _ in ATTN_BLOCKS)
TBL_ROWS = len(ATTN_BLOCKS) * HEADS_PER_GROUP * BLOCK_Q_MAX


def _attn_kernel(q_ref, k_ref, v_ref, z_ref, tbl_ref, o_ref, *, rows):
    lane_head = lax.broadcasted_iota(jnp.int32, (1, GROUP_W), 1) // HEAD_DIM
    head_lanes = [lane_head == h for h in range(HEADS_PER_GROUP)]

    def span_rows(ref, k0, sc):
        return jnp.concatenate(
            [ref[pl.ds(pl.multiple_of(k0 + j * GRID_W + sc, 16), SPAN_C), :] for j in range(WIN_R)], axis=0)

    def body(i, carry):
        chains = []
        for u in range(ATTN_ROWS_PER_STEP):
            r = i * ATTN_ROWS_PER_STEP + u
            rs = jnp.clip(r - WIN_R // 2, 0, rows - WIN_R)
            dr0 = rs - r + (WIN_R - 1)
            q0 = pl.multiple_of(r * GRID_W, GRID_W)
            k0 = pl.multiple_of(rs * GRID_W, GRID_W)
            qf = q_ref[pl.ds(q0, GRID_W), :].astype(F32)
            for bi, (qc0, nq, sc) in enumerate(ATTN_BLOCKS):
                chains.append((dr0, k0, qf[qc0:qc0 + nq], nq, sc, bi * HEADS_PER_GROUP * BLOCK_Q_MAX))

        scores = []
        for dr0, k0, qb, nq, sc, tbl_row in chains:
            lhs = jnp.concatenate([jnp.where(hl, qb, 0.0) for hl in head_lanes], axis=0).astype(BF16)
            scores.append(lax.dot_general(lhs, span_rows(k_ref, k0, sc), (((1,), (1,)), ((), ())),
                                          preferred_element_type=F32))

        probs = []
        for (dr0, k0, qb, nq, sc, tbl_row), s in zip(chains, scores):
            s = s + jnp.concatenate(
                [tbl_ref[dr0, tbl_row + h * BLOCK_Q_MAX:tbl_row + h * BLOCK_Q_MAX + nq, :]
                 for h in range(HEADS_PER_GROUP)], axis=0)
            m = jnp.max(s, axis=-1, keepdims=True)
            p = jnp.exp2(s - m)
            l = jnp.sum(p, axis=-1, keepdims=True)
            probs.append((p.astype(BF16), l))

        blocks = []
        for (dr0, k0, qb, nq, sc, tbl_row), (p, l) in zip(chains, probs):
            pv = jnp.dot(p, span_rows(v_ref, k0, sc), preferred_element_type=F32) * (1.0 / l)
            ob = pv[:nq]
            for h in range(1, HEADS_PER_GROUP):
                ob = jnp.where(head_lanes[h], pv[h * nq:(h + 1) * nq], ob)
            blocks.append(ob)

        for u in range(ATTN_ROWS_PER_STEP):
            q0 = pl.multiple_of((i * ATTN_ROWS_PER_STEP + u) * GRID_W, GRID_W)
            o = jnp.concatenate(blocks[u * len(ATTN_BLOCKS):(u + 1) * len(ATTN_BLOCKS)], axis=0)
            o = o * z_ref[pl.ds(q0, GRID_W), :].astype(F32)
            o_ref[pl.ds(q0, GRID_W), :] = o.astype(BF16)
        return carry

    lax.fori_loop(0, rows // ATTN_ROWS_PER_STEP, body, 0)


def _attn(proj, bias_tbl, n_batch, seq):
    rows = seq // GRID_W
    assert rows % ATTN_ROWS_PER_STEP == 0 and rows >= WIN_R
    col = lambda off: (lambda b, g: (off // GROUP_W + g, b, 0))
    blk = (None, seq, GROUP_W)
    return pl.pallas_call(
        functools.partial(_attn_kernel, rows=rows),
        grid=(n_batch, N_GROUPS),
        in_specs=[
            pl.BlockSpec(blk, col(COL_Q)),
            pl.BlockSpec(blk, col(COL_K)),
            pl.BlockSpec(blk, col(COL_V)),
            pl.BlockSpec(blk, col(COL_ATTN_Z)),
            pl.BlockSpec((None, WIN_R, TBL_ROWS, SPAN_KEYS), lambda b, g: (g, 0, 0, 0)),
        ],
        out_specs=pl.BlockSpec((seq, GROUP_W), lambda b, g: (b, g)),
        out_shape=jax.ShapeDtypeStruct((n_batch * seq, ATTN_WIDTH), BF16),
        compiler_params=pltpu.CompilerParams(
            dimension_semantics=("parallel", "parallel"), vmem_limit_bytes=VMEM_LIMIT_BYTES),
        name="attn",
    )(proj, proj, proj, proj, bias_tbl)


def _bias_selectors():
    n_rel = 2 * WIN_C - 1
    sel_rel = np.zeros((n_rel, len(ATTN_BLOCKS), BLOCK_Q_MAX, SPAN_C), np.float32)
    for bi, (qc0, nq, sc) in enumerate(ATTN_BLOCKS):
        for q in range(nq):
            c = qc0 + q
            c_start = min(max(c - WIN_C // 2, 0), GRID_W - WIN_C)
            for kc in range(c_start, c_start + WIN_C):
                sel_rel[kc - c + WIN_C - 1, bi, q, kc - sc] = 1.0
    sel_row = np.zeros((2 * WIN_R - 1, WIN_R, WIN_R), np.float32)
    for dr0 in range(WIN_R):
        for j in range(WIN_R):
            sel_row[dr0 + j, dr0, j] = 1.0
    return sel_rel.reshape(n_rel, -1), sel_row.reshape(2 * WIN_R - 1, -1)


def _bias_table(rpb):
    sel_rel, sel_row = _bias_selectors()
    exact = lax.Precision.HIGHEST
    by_col = jnp.einsum('hdm,mx->hdx', rpb, sel_rel, precision=exact)
    by_row = jnp.einsum('hdx,dy->hyx', by_col, sel_row, precision=exact)
    in_win = jnp.asarray(sel_rel.sum(axis=0) > 0)
    t = jnp.where(in_win, by_row * LOG2_E, MASK_BIAS).astype(F32)
    t = t.reshape(N_GROUPS, HEADS_PER_GROUP, WIN_R, WIN_R, len(ATTN_BLOCKS), BLOCK_Q_MAX, SPAN_C)
    t = t.transpose(0, 2, 4, 1, 5, 3, 6)
    return t.reshape(N_GROUPS, WIN_R, TBL_ROWS, SPAN_KEYS)


POOL_CHUNK = 256
POOL_HALO = 16
POOL_SPAN = POOL_CHUNK + 2 * POOL_HALO


def _pool_kernel(u_ref, z_ref, w_ref, s_ref, o_ref, d_ref, *, seq):
    half = jnp.left_shift(1, pl.program_id(1))
    rel0 = (lax.broadcasted_iota(jnp.int32, (POOL_CHUNK, POOL_SPAN), 1)
            - lax.broadcasted_iota(jnp.int32, (POOL_CHUNK, POOL_SPAN), 0))
    bands = {}
    for ci in range(seq // POOL_CHUNK):
        t0 = ci * POOL_CHUNK
        start = min(max(t0 - POOL_HALO, 0), seq - POOL_SPAN)
        if start - t0 not in bands:
            rel = rel0 + (start - t0)
            bands[start - t0] = ((rel >= -half) & (rel < half)).astype(BF16)
        window_sum = jnp.dot(bands[start - t0], u_ref[start:start + POOL_SPAN, :], preferred_element_type=F32)
        t = t0 + lax.broadcasted_iota(jnp.int32, (POOL_CHUNK, 1), 0)
        cnt = jnp.minimum(t + half - 1, seq - 1) - jnp.maximum(t - half, 0) + 1
        d = window_sum / cnt.astype(F32) - u_ref[t0:t0 + POOL_CHUNK, :].astype(F32)
        d_ref[t0:t0 + POOL_CHUNK, :] = d.astype(BF16)
    w_mat = w_ref[...]
    scale = s_ref[...]
    for ci in range(seq // POOL_CHUNK):
        t0 = ci * POOL_CHUNK
        y = jnp.dot(d_ref[t0:t0 + POOL_CHUNK, :], w_mat, preferred_element_type=F32)
        y = y * scale * z_ref[t0:t0 + POOL_CHUNK, :].astype(F32)
        o_ref[t0:t0 + POOL_CHUNK, :] = y.astype(BF16)


def _pool(proj, pool_w_bf, pool_scale2, n_batch, seq):
    assert seq % POOL_CHUNK == 0 and seq >= POOL_SPAN and N_GROUPS == 4
    blk = (None, seq, GROUP_W)
    return pl.pallas_call(
        functools.partial(_pool_kernel, seq=seq),
        grid=(n_batch, N_GROUPS),
        in_specs=[
            pl.BlockSpec(blk, lambda b, g: (COL_POOL_U // GROUP_W + g, b, 0)),
            pl.BlockSpec(blk, lambda b, g: (COL_POOL_Z // GROUP_W + g, b, 0)),
            pl.BlockSpec((None, GROUP_W, GROUP_W), lambda b, g: (g, 0, 0)),
            pl.BlockSpec((1, GROUP_W), lambda b, g: (0, g)),
        ],
        out_specs=pl.BlockSpec((seq, GROUP_W), lambda b, g: (b, g)),
        out_shape=jax.ShapeDtypeStruct((n_batch * seq, POOL_WIDTH), BF16),
        scratch_shapes=[pltpu.VMEM((seq, GROUP_W), BF16)],
        compiler_params=pltpu.CompilerParams(
            dimension_semantics=("parallel", "parallel"), vmem_limit_bytes=VMEM_LIMIT_BYTES),
        name="pool",
    )(proj, proj, pool_w_bf, pool_scale2)


OUT_TM = 512
OUT_CHUNK = 512


def _out_proj_kernel(x_ref, py_ref, ay_ref, gp_ref, ga_ref, gate_ref, wpu_ref, wau_ref, wo_ref, o_ref, m_ref):
    py = py_ref[...]
    ay = ay_ref[...]
    for ci in range(D_MODEL // OUT_CHUNK):
        cs = slice(ci * OUT_CHUNK, (ci + 1) * OUT_CHUNK)
        pooled = jnp.dot(py, wpu_ref[:, cs], preferred_element_type=F32)
        attended = jnp.dot(ay, wau_ref[:, cs], preferred_element_type=F32)
        slabs = range(ci * (OUT_CHUNK // GROUP_W), (ci + 1) * (OUT_CHUNK // GROUP_W))
        g_pool = jnp.concatenate([gp_ref[sl] for sl in slabs], axis=1).astype(F32)
        g_attn = jnp.concatenate([ga_ref[sl] for sl in slabs], axis=1).astype(F32)
        m_ref[:, cs] = (g_pool * pooled + g_attn * attended).astype(BF16)
    merged = m_ref[...]
    for ci in range(D_MODEL // OUT_CHUNK):
        cs = slice(ci * OUT_CHUNK, (ci + 1) * OUT_CHUNK)
        y = jnp.dot(merged, wo_ref[:, cs], preferred_element_type=F32)
        o_ref[:, cs] = x_ref[:, cs] + gate_ref[:, cs] * y


def _out_proj(x2, pool_y, attn_y, proj, ada3, b_off, seq, wpu_bf, wau_bf, wo_bf):
    m = x2.shape[0]
    assert seq % OUT_TM == 0
    bpb = seq // OUT_TM
    resident = lambda shape: pl.BlockSpec(shape, lambda i: (0, 0), pipeline_mode=pl.Buffered(1))
    return pl.pallas_call(
        _out_proj_kernel,
        grid=(m // OUT_TM,),
        in_specs=[
            pl.BlockSpec((OUT_TM, D_MODEL), lambda i: (i, 0)),
            pl.BlockSpec((OUT_TM, POOL_WIDTH), lambda i: (i, 0)),
            pl.BlockSpec((OUT_TM, ATTN_WIDTH), lambda i: (i, 0)),
            pl.BlockSpec((D_MODEL // GROUP_W, OUT_TM, GROUP_W), lambda i: (COL_G_POOL // D_MODEL, i, 0)),
            pl.BlockSpec((D_MODEL // GROUP_W, OUT_TM, GROUP_W), lambda i: (COL_G_ATTN // D_MODEL, i, 0)),
            pl.BlockSpec((None, 1, D_MODEL), lambda i: (b_off + i // bpb, 0, 2)),
            resident((POOL_WIDTH, D_MODEL)),
            resident((ATTN_WIDTH, D_MODEL)),
            resident((D_MODEL, D_MODEL)),
        ],
        out_specs=pl.BlockSpec((OUT_TM, D_MODEL), lambda i: (i, 0)),
        out_shape=jax.ShapeDtypeStruct((m, D_MODEL), F32),
        scratch_shapes=[pltpu.VMEM((OUT_TM, D_MODEL), BF16)],
        compiler_params=pltpu.CompilerParams(
            dimension_semantics=("parallel",), vmem_limit_bytes=VMEM_LIMIT_BYTES),
        name="out_proj",
    )(x2, pool_y, attn_y, proj, proj, ada3, wpu_bf, wau_bf, wo_bf)


def kernel(x_prompt, x_sample, c_prompt, c_sample, w_ada, b_ada, norm_g, w_in, pool_w, pool_scale,
           q_norm_g, k_norm_g, rpb, w_pool_up, w_attn_up, w_o):
    assert w_ada.shape[0] == 1, "single-layer trunk"
    nb_prompt = x_prompt.shape[0]
    c_all = jnp.concatenate([c_prompt, c_sample], axis=0)
    ada = _ada(c_all, w_ada[0], b_ada[0])
    ada3 = ada.reshape(ada.shape[0], 1, 3 * D_MODEL)

    w_in_bf = w_in[0].astype(BF16)
    pool_w_bf = pool_w[0].astype(BF16)
    wpu_bf = w_pool_up[0].astype(BF16)
    wau_bf = w_attn_up[0].astype(BF16)
    wo_bf = w_o[0].astype(BF16)
    norm_g2 = norm_g[0].reshape(1, D_MODEL)
    pool_scale2 = pool_scale[0].reshape(1, POOL_WIDTH)
    qg_t = jnp.tile(q_norm_g[0] * (HEAD_DIM ** -0.5 * LOG2_E), N_HEADS)
    kg_t = jnp.tile(k_norm_g[0], N_HEADS)
    head_gains = jnp.stack([qg_t, kg_t]).reshape(2, 1, ATTN_WIDTH)
    bias_tbl = _bias_table(rpb[0])

    def trunk(x, b_off):
        n_batch, seq, _ = x.shape
        x2 = x.reshape(n_batch * seq, D_MODEL)
        proj = _in_proj(x2, ada3, b_off, seq, norm_g2, w_in_bf, head_gains)
        attn_y = _attn(proj, bias_tbl, n_batch, seq)
        pool_y = _pool(proj, pool_w_bf, pool_scale2, n_batch, seq)
        y = _out_proj(x2, pool_y, attn_y, proj, ada3, b_off, seq, wpu_bf, wau_bf, wo_bf)
        return y.reshape(n_batch, seq, D_MODEL)

    return (trunk(x_prompt, 0), trunk(x_sample, nb_prompt))
```

```python
import functools

import jax
import jax.numpy as jnp
import numpy as np
from jax import lax
from jax.experimental import pallas as pl
from jax.experimental.pallas import tpu as pltpu

D_MODEL = 2048
GRID_W = 64
POOL_WIDTH = 1024
GROUP_W = 256
N_GROUPS = 4
HEAD_DIM = 64
N_HEADS = 16
ATTN_WIDTH = 1024
WIN_R = 8
WIN_C = 16
IN_WIDTH = 10240
NORM_EPS = 1e-6
MASK_BIAS = -1e30
LOG2_E = 1.4426950408889634

COL_POOL_U, COL_POOL_Z, COL_Q, COL_K, COL_V, COL_ATTN_Z, COL_G_POOL, COL_G_ATTN = (
    0, 1024, 2048, 3072, 4096, 5120, 6144, 8192)
N_SLABS = IN_WIDTH // GROUP_W

VMEM_LIMIT_BYTES = 56 * 1024 * 1024

BF16 = jnp.bfloat16
F32 = jnp.float32


def _silu(x):
    return x * jax.nn.sigmoid(x)


ADA_TN = 512


def _ada_kernel(c_ref, w_ref, b_ref, o_ref):
    c = c_ref[...]
    a = _silu(c).astype(BF16)
    o_ref[...] = jnp.dot(a, w_ref[...].astype(BF16), preferred_element_type=F32) + b_ref[...]


def _ada(c_all, w_ada, b_ada):
    nb = c_all.shape[0]
    n = w_ada.shape[1]
    return pl.pallas_call(
        _ada_kernel,
        grid=(n // ADA_TN,),
        in_specs=[
            pl.BlockSpec((nb, D_MODEL), lambda j: (0, 0)),
            pl.BlockSpec((D_MODEL, ADA_TN), lambda j: (0, j)),
            pl.BlockSpec((1, ADA_TN), lambda j: (0, j)),
        ],
        out_specs=pl.BlockSpec((nb, ADA_TN), lambda j: (0, j)),
        out_shape=jax.ShapeDtypeStruct((nb, n), F32),
        compiler_params=pltpu.CompilerParams(dimension_semantics=("parallel",)),
        name="ada",
    )(c_all, w_ada, b_ada.reshape(1, n))


IN_TM = 1024
IN_TN = 2048
IN_GROUP_W = 1024
IN_ROW_CHUNK = 256
IN_NORM_ROWS = 16

TILE_POOL, TILE_QK, TILE_V_ATTN_Z, TILE_GATES = 0, 1, 2, 3


def _head_mean_square(acc):
    first_head = lax.broadcasted_iota(jnp.int32, (1, 128), 1) < HEAD_DIM
    parts = []
    for i in range(IN_GROUP_W // 128):
        sq = acc[:, i * 128:(i + 1) * 128]
        sq = sq * sq
        lo = jnp.sum(jnp.where(first_head, sq, 0.0), axis=-1, keepdims=True)
        hi = jnp.sum(jnp.where(first_head, 0.0, sq), axis=-1, keepdims=True)
        parts.append(jnp.where(first_head, lo, hi))
    return jnp.concatenate(parts, axis=1) * (1.0 / HEAD_DIM)


def _in_proj_kernel(x_ref, shift_ref, scl_ref, g_ref, w_ref, hg_ref, o_ref, h_ref):
    j = pl.program_id(1)

    def normalize(rows0):
        gm = g_ref[...] * (1.0 + scl_ref[...])
        shift = shift_ref[...]
        for s in range(IN_ROW_CHUNK // IN_NORM_ROWS):
            rows = slice(rows0 + s * IN_NORM_ROWS, rows0 + (s + 1) * IN_NORM_ROWS)
            x = x_ref[rows, :]
            ms = jnp.mean(x * x, axis=-1, keepdims=True)
            h_ref[rows, :] = (x * lax.rsqrt(ms + NORM_EPS) * gm + shift).astype(BF16)

    def run(epilogues, with_norm=False):
        for c in range(IN_TM // IN_ROW_CHUNK):
            rows = slice(c * IN_ROW_CHUNK, (c + 1) * IN_ROW_CHUNK)
            if with_norm:
                normalize(c * IN_ROW_CHUNK)
            for half, epilogue in enumerate(epilogues):
                cols = slice(half * IN_GROUP_W, (half + 1) * IN_GROUP_W)
                acc = jnp.dot(h_ref[rows, :], w_ref[:, cols], preferred_element_type=F32)
                out = epilogue(acc).astype(BF16)
                for sl in range(IN_GROUP_W // GROUP_W):
                    o_ref[half * (IN_GROUP_W // GROUP_W) + sl, rows, :] = out[:, sl * GROUP_W:(sl + 1) * GROUP_W]

    def head_norm(which):
        return lambda acc: acc * lax.rsqrt(_head_mean_square(acc) + NORM_EPS) * hg_ref[which]

    plain = lambda acc: acc

    @pl.when(j == TILE_POOL)
    def _():
        run((plain, _silu), with_norm=True)

    @pl.when(j == TILE_QK)
    def _():
        run((head_norm(0), head_norm(1)))

    @pl.when(j == TILE_V_ATTN_Z)
    def _():
        run((plain, _silu))

    @pl.when(j >= TILE_GATES)
    def _():
        run((jax.nn.sigmoid, jax.nn.sigmoid))


def _in_proj(x2, ada3, b_off, seq, norm_g, w_in_bf, head_gains):
    m = x2.shape[0]
    assert seq % IN_TM == 0 and IN_GROUP_W == ATTN_WIDTH == POOL_WIDTH
    bpb = seq // IN_TM
    return pl.pallas_call(
        _in_proj_kernel,
        grid=(m // IN_TM, IN_WIDTH // IN_TN),
        in_specs=[
            pl.BlockSpec((IN_TM, D_MODEL), lambda i, j: (i, 0)),
            pl.BlockSpec((None, 1, D_MODEL), lambda i, j: (b_off + i // bpb, 0, 0)),
            pl.BlockSpec((None, 1, D_MODEL), lambda i, j: (b_off + i // bpb, 0, 1)),
            pl.BlockSpec((1, D_MODEL), lambda i, j: (0, 0)),
            pl.BlockSpec((D_MODEL, IN_TN), lambda i, j: (0, j)),
            pl.BlockSpec((2, 1, IN_GROUP_W), lambda i, j: (0, 0, 0)),
        ],
        out_specs=pl.BlockSpec((IN_TN // GROUP_W, IN_TM, GROUP_W), lambda i, j: (j, i, 0)),
        out_shape=jax.ShapeDtypeStruct((N_SLABS, m, GROUP_W), BF16),
        scratch_shapes=[pltpu.VMEM((IN_TM, D_MODEL), BF16)],
        compiler_params=pltpu.CompilerParams(
            dimension_semantics=("parallel", "arbitrary"), vmem_limit_bytes=VMEM_LIMIT_BYTES),
        name="in_proj",
    )(x2, ada3, ada3, norm_g, w_in_bf, head_gains)


ATTN_ROWS_PER_STEP = 8
HEADS_PER_GROUP = N_HEADS // N_GROUPS
SPAN_C = 32
STRIP_C = 16
ATTN_BLOCKS = ((0, 24, 0), (24, 16, 16), (40, 24, 32))
SPAN_KEYS = WIN_R * SPAN_C
BLOCK_Q_MAX = max(nq for _, nq, _ in ATTN_BLOCKS)
TBL_ROWS = len(ATTN_BLOCKS) * HEADS_PER_GROUP * BLOCK_Q_MAX


def _attn_kernel(q_ref, k_ref, v_ref, z_ref, tbl_ref, o_ref, *, rows):
    lane_head = lax.broadcasted_iota(jnp.int32, (1, GROUP_W), 1) // HEAD_DIM
    head_lanes = [lane_head == h for h in range(HEADS_PER_GROUP)]

    def key_strips(ref, k0, strips):
        return jnp.concatenate(
            [ref[pl.ds(pl.multiple_of(k0 + j * GRID_W + cb * STRIP_C, STRIP_C), STRIP_C), :]
             for cb in strips for j in range(WIN_R)], axis=0)

    def body(i, carry):
        row_ctx = []
        for u in range(ATTN_ROWS_PER_STEP):
            r = i * ATTN_ROWS_PER_STEP + u
            rs = jnp.clip(r - WIN_R // 2, 0, rows - WIN_R)
            row_ctx.append((rs - r + (WIN_R - 1), pl.multiple_of(r * GRID_W, GRID_W), pl.multiple_of(rs * GRID_W, GRID_W)))

        scores = []
        for dr0, q0, k0 in row_ctx:
            qf = q_ref[pl.ds(q0, GRID_W), :].astype(F32)
            lhs = jnp.concatenate(
                [jnp.where(hl, qf[qc0:qc0 + nq], 0.0) for qc0, nq, _ in ATTN_BLOCKS for hl in head_lanes],
                axis=0).astype(BF16)
            scores.append(lax.dot_general(lhs, key_strips(k_ref, k0, range(GRID_W // STRIP_C)),
                                          (((1,), (1,)), ((), ())), preferred_element_type=F32))

        probs = []
        for (dr0, q0, k0), s_row in zip(row_ctx, scores):
            row0 = 0
            for bi, (qc0, nq, sc) in enumerate(ATTN_BLOCKS):
                lane0 = (sc // STRIP_C) * WIN_R * STRIP_C
                s = s_row[row0:row0 + HEADS_PER_GROUP * nq, lane0:lane0 + SPAN_KEYS]
                row0 += HEADS_PER_GROUP * nq
                tbl_row = bi * HEADS_PER_GROUP * BLOCK_Q_MAX
                s = s + jnp.concatenate(
                    [tbl_ref[dr0, tbl_row + h * BLOCK_Q_MAX:tbl_row + h * BLOCK_Q_MAX + nq, :]
                     for h in range(HEADS_PER_GROUP)], axis=0)
                m = jnp.max(s, axis=-1, keepdims=True)
                p = jnp.exp2(s - m)
                l = jnp.sum(p, axis=-1, keepdims=True)
                probs.append((p.astype(BF16), l))

        blocks = []
        pi = 0
        for dr0, q0, k0 in row_ctx:
            for qc0, nq, sc in ATTN_BLOCKS:
                p, l = probs[pi]
                pi += 1
                strips = (sc // STRIP_C, sc // STRIP_C + 1)
                pv = jnp.dot(p, key_strips(v_ref, k0, strips), preferred_element_type=F32) * (1.0 / l)
                ob = pv[:nq]
                for h in range(1, HEADS_PER_GROUP):
                    ob = jnp.where(head_lanes[h], pv[h * nq:(h + 1) * nq], ob)
                blocks.append(ob)

        for u, (dr0, q0, k0) in enumerate(row_ctx):
            o = jnp.concatenate(blocks[u * len(ATTN_BLOCKS):(u + 1) * len(ATTN_BLOCKS)], axis=0)
            o = o * z_ref[pl.ds(q0, GRID_W), :].astype(F32)
            o_ref[pl.ds(q0, GRID_W), :] = o.astype(BF16)
        return carry

    lax.fori_loop(0, rows // ATTN_ROWS_PER_STEP, body, 0)


def _attn(proj, bias_tbl, n_batch, seq):
    rows = seq // GRID_W
    assert rows % ATTN_ROWS_PER_STEP == 0 and rows >= WIN_R
    col = lambda off: (lambda b, g: (off // GROUP_W + g, b, 0))
    blk = (None, seq, GROUP_W)
    return pl.pallas_call(
        functools.partial(_attn_kernel, rows=rows),
        grid=(n_batch, N_GROUPS),
        in_specs=[
            pl.BlockSpec(blk, col(COL_Q)),
            pl.BlockSpec(blk, col(COL_K)),
            pl.BlockSpec(blk, col(COL_V)),
            pl.BlockSpec(blk, col(COL_ATTN_Z)),
            pl.BlockSpec((None, WIN_R, TBL_ROWS, SPAN_KEYS), lambda b, g: (g, 0, 0, 0)),
        ],
        out_specs=pl.BlockSpec((seq, GROUP_W), lambda b, g: (b, g)),
        out_shape=jax.ShapeDtypeStruct((n_batch * seq, ATTN_WIDTH), BF16),
        compiler_params=pltpu.CompilerParams(
            dimension_semantics=("parallel", "parallel"), vmem_limit_bytes=VMEM_LIMIT_BYTES),
        name="attn",
    )(proj, proj, proj, proj, bias_tbl)


def _bias_selectors():
    n_rel = 2 * WIN_C - 1
    sel_rel = np.zeros((n_rel, len(ATTN_BLOCKS), BLOCK_Q_MAX, SPAN_C), np.float32)
    for bi, (qc0, nq, sc) in enumerate(ATTN_BLOCKS):
        for q in range(nq):
            c = qc0 + q
            c_start = min(max(c - WIN_C // 2, 0), GRID_W - WIN_C)
            for kc in range(c_start, c_start + WIN_C):
                sel_rel[kc - c + WIN_C - 1, bi, q, kc - sc] = 1.0
    sel_row = np.zeros((2 * WIN_R - 1, WIN_R, WIN_R), np.float32)
    for dr0 in range(WIN_R):
        for j in range(WIN_R):
            sel_row[dr0 + j, dr0, j] = 1.0
    return sel_rel.reshape(n_rel, -1), sel_row.reshape(2 * WIN_R - 1, -1)


def _bias_table(rpb):
    sel_rel, sel_row = _bias_selectors()
    exact = lax.Precision.HIGHEST
    by_col = jnp.einsum('hdm,mx->hdx', rpb, sel_rel, precision=exact)
    by_row = jnp.einsum('hdx,dy->hyx', by_col, sel_row, precision=exact)
    in_win = jnp.asarray(sel_rel.sum(axis=0) > 0)
    t = jnp.where(in_win, by_row * LOG2_E, MASK_BIAS).astype(F32)
    t = t.reshape(N_GROUPS, HEADS_PER_GROUP, WIN_R, WIN_R, len(ATTN_BLOCKS), BLOCK_Q_MAX, SPAN_C // STRIP_C, STRIP_C)
    t = t.transpose(0, 2, 4, 1, 5, 6, 3, 7)
    return t.reshape(N_GROUPS, WIN_R, TBL_ROWS, SPAN_KEYS)


POOL_CHUNK = 256
POOL_HALO = 16
POOL_SPAN = POOL_CHUNK + 2 * POOL_HALO


def _pool_kernel(u_ref, z_ref, w_ref, s_ref, o_ref, d_ref, *, seq):
    half = jnp.left_shift(1, pl.program_id(1))
    rel0 = (lax.broadcasted_iota(jnp.int32, (POOL_CHUNK, POOL_SPAN), 1)
            - lax.broadcasted_iota(jnp.int32, (POOL_CHUNK, POOL_SPAN), 0))
    bands = {}
    for ci in range(seq // POOL_CHUNK):
        t0 = ci * POOL_CHUNK
        start = min(max(t0 - POOL_HALO, 0), seq - POOL_SPAN)
        if start - t0 not in bands:
            rel = rel0 + (start - t0)
            bands[start - t0] = ((rel >= -half) & (rel < half)).astype(BF16)
        window_sum = jnp.dot(bands[start - t0], u_ref[start:start + POOL_SPAN, :], preferred_element_type=F32)
        t = t0 + lax.broadcasted_iota(jnp.int32, (POOL_CHUNK, 1), 0)
        cnt = jnp.minimum(t + half - 1, seq - 1) - jnp.maximum(t - half, 0) + 1
        d = window_sum / cnt.astype(F32) - u_ref[t0:t0 + POOL_CHUNK, :].astype(F32)
        d_ref[t0:t0 + POOL_CHUNK, :] = d.astype(BF16)
    w_mat = w_ref[...]
    scale = s_ref[...]
    for ci in range(seq // POOL_CHUNK):
        t0 = ci * POOL_CHUNK
        y = jnp.dot(d_ref[t0:t0 + POOL_CHUNK, :], w_mat, preferred_element_type=F32)
        y = y * scale * z_ref[t0:t0 + POOL_CHUNK, :].astype(F32)
        o_ref[t0:t0 + POOL_CHUNK, :] = y.astype(BF16)


def _pool(proj, pool_w_bf, pool_scale2, n_batch, seq):
    assert seq % POOL_CHUNK == 0 and seq >= POOL_SPAN and N_GROUPS == 4
    blk = (None, seq, GROUP_W)
    return pl.pallas_call(
        functools.partial(_pool_kernel, seq=seq),
        grid=(n_batch, N_GROUPS),
        in_specs=[
            pl.BlockSpec(blk, lambda b, g: (COL_POOL_U // GROUP_W + g, b, 0)),
            pl.BlockSpec(blk, lambda b, g: (COL_POOL_Z // GROUP_W + g, b, 0)),
            pl.BlockSpec((None, GROUP_W, GROUP_W), lambda b, g: (g, 0, 0)),
            pl.BlockSpec((1, GROUP_W), lambda b, g: (0, g)),
        ],
        out_specs=pl.BlockSpec((seq, GROUP_W), lambda b, g: (b, g)),
        out_shape=jax.ShapeDtypeStruct((n_batch * seq, POOL_WIDTH), BF16),
        scratch_shapes=[pltpu.VMEM((seq, GROUP_W), BF16)],
        compiler_params=pltpu.CompilerParams(
            dimension_semantics=("parallel", "parallel"), vmem_limit_bytes=VMEM_LIMIT_BYTES),
        name="pool",
    )(proj, proj, pool_w_bf, pool_scale2)


OUT_TM = 512
OUT_CHUNK = 512


def _out_proj_kernel(x_ref, py_ref, ay_ref, gp_ref, ga_ref, gate_ref, wpu_ref, wau_ref, wo_ref, o_ref, m_ref):
    py = py_ref[...]
    ay = ay_ref[...]
    for ci in range(D_MODEL // OUT_CHUNK):
        cs = slice(ci * OUT_CHUNK, (ci + 1) * OUT_CHUNK)
        pooled = jnp.dot(py, wpu_ref[:, cs], preferred_element_type=F32)
        attended = jnp.dot(ay, wau_ref[:, cs], preferred_element_type=F32)
        slabs = range(ci * (OUT_CHUNK // GROUP_W), (ci + 1) * (OUT_CHUNK // GROUP_W))
        g_pool = jnp.concatenate([gp_ref[sl] for sl in slabs], axis=1).astype(F32)
        g_attn = jnp.concatenate([ga_ref[sl] for sl in slabs], axis=1).astype(F32)
        m_ref[:, cs] = (g_pool * pooled + g_attn * attended).astype(BF16)
    merged = m_ref[...]
    for ci in range(D_MODEL // OUT_CHUNK):
        cs = slice(ci * OUT_CHUNK, (ci + 1) * OUT_CHUNK)
        y = jnp.dot(merged, wo_ref[:, cs], preferred_element_type=F32)
        o_ref[:, cs] = x_ref[:, cs] + gate_ref[:, cs] * y


def _out_proj(x2, pool_y, attn_y, proj, ada3, b_off, seq, wpu_bf, wau_bf, wo_bf):
    m = x2.shape[0]
    assert seq % OUT_TM == 0
    bpb = seq // OUT_TM
    resident = lambda shape: pl.BlockSpec(shape, lambda i: (0, 0), pipeline_mode=pl.Buffered(1))
    return pl.pallas_call(
        _out_proj_kernel,
        grid=(m // OUT_TM,),
        in_specs=[
            pl.BlockSpec((OUT_TM, D_MODEL), lambda i: (i, 0)),
            pl.BlockSpec((OUT_TM, POOL_WIDTH), lambda i: (i, 0)),
            pl.BlockSpec((OUT_TM, ATTN_WIDTH), lambda i: (i, 0)),
            pl.BlockSpec((D_MODEL // GROUP_W, OUT_TM, GROUP_W), lambda i: (COL_G_POOL // D_MODEL, i, 0)),
            pl.BlockSpec((D_MODEL // GROUP_W, OUT_TM, GROUP_W), lambda i: (COL_G_ATTN // D_MODEL, i, 0)),
            pl.BlockSpec((None, 1, D_MODEL), lambda i: (b_off + i // bpb, 0, 2)),
            resident((POOL_WIDTH, D_MODEL)),
            resident((ATTN_WIDTH, D_MODEL)),
            resident((D_MODEL, D_MODEL)),
        ],
        out_specs=pl.BlockSpec((OUT_TM, D_MODEL), lambda i: (i, 0)),
        out_shape=jax.ShapeDtypeStruct((m, D_MODEL), F32),
        scratch_shapes=[pltpu.VMEM((OUT_TM, D_MODEL), BF16)],
        compiler_params=pltpu.CompilerParams(
            dimension_semantics=("parallel",), vmem_limit_bytes=VMEM_LIMIT_BYTES),
        name="out_proj",
    )(x2, pool_y, attn_y, proj, proj, ada3, wpu_bf, wau_bf, wo_bf)


def kernel(x_prompt, x_sample, c_prompt, c_sample, w_ada, b_ada, norm_g, w_in, pool_w, pool_scale,
           q_norm_g, k_norm_g, rpb, w_pool_up, w_attn_up, w_o):
    assert w_ada.shape[0] == 1, "single-layer trunk"
    nb_prompt = x_prompt.shape[0]
    c_all = jnp.concatenate([c_prompt, c_sample], axis=0)
    ada = _ada(c_all, w_ada[0], b_ada[0])
    ada3 = ada.reshape(ada.shape[0], 1, 3 * D_MODEL)

    w_in_bf = w_in[0].astype(BF16)
    pool_w_bf = pool_w[0].astype(BF16)
    wpu_bf = w_pool_up[0].astype(BF16)
    wau_bf = w_attn_up[0].astype(BF16)
    wo_bf = w_o[0].astype(BF16)
    norm_g2 = norm_g[0].reshape(1, D_MODEL)
    pool_scale2 = pool_scale[0].reshape(1, POOL_WIDTH)
    qg_t = jnp.tile(q_norm_g[0] * (HEAD_DIM ** -0.5 * LOG2_E), N_HEADS)
    kg_t = jnp.tile(k_norm_g[0], N_HEADS)
    head_gains = jnp.stack([qg_t, kg_t]).reshape(2, 1, ATTN_WIDTH)
    bias_tbl = _bias_table(rpb[0])

    def trunk(x, b_off):
        n_batch, seq, _ = x.shape
        x2 = x.reshape(n_batch * seq, D_MODEL)
        proj = _in_proj(x2, ada3, b_off, seq, norm_g2, w_in_bf, head_gains)
        attn_y = _attn(proj, bias_tbl, n_batch, seq)
        pool_y = _pool(proj, pool_w_bf, pool_scale2, n_batch, seq)
        y = _out_proj(x2, pool_y, attn_y, proj, ada3, b_off, seq, wpu_bf, wau_bf, wo_bf)
        return y.reshape(n_batch, seq, D_MODEL)

    return (trunk(x_prompt, 0), trunk(x_sample, nb_prompt))
```

```python
import functools

import jax
import jax.numpy as jnp
import numpy as np
from jax import lax
from jax.experimental import pallas as pl
from jax.experimental.pallas import tpu as pltpu

D_MODEL = 2048
GRID_W = 64
POOL_WIDTH = 1024
GROUP_W = 256
N_GROUPS = 4
HEAD_DIM = 64
N_HEADS = 16
ATTN_WIDTH = 1024
WIN_R = 8
WIN_C = 16
IN_WIDTH = 10240
NORM_EPS = 1e-6
MASK_BIAS = -1e30
LOG2_E = 1.4426950408889634

COL_POOL_U, COL_POOL_Z, COL_Q, COL_K, COL_V, COL_ATTN_Z, COL_G_POOL, COL_G_ATTN = (
    0, 1024, 2048, 3072, 4096, 5120, 6144, 8192)
N_SLABS = IN_WIDTH // GROUP_W

VMEM_LIMIT_BYTES = 56 * 1024 * 1024

BF16 = jnp.bfloat16
F32 = jnp.float32


def _silu(x):
    return x * jax.nn.sigmoid(x)


ADA_TN = 512


def _ada_kernel(c_ref, w_ref, b_ref, o_ref):
    c = c_ref[...]
    a = _silu(c).astype(BF16)
    o_ref[...] = jnp.dot(a, w_ref[...].astype(BF16), preferred_element_type=F32) + b_ref[...]


def _ada(c_all, w_ada, b_ada):
    nb = c_all.shape[0]
    n = w_ada.shape[1]
    return pl.pallas_call(
        _ada_kernel,
        grid=(n // ADA_TN,),
        in_specs=[
            pl.BlockSpec((nb, D_MODEL), lambda j: (0, 0)),
            pl.BlockSpec((D_MODEL, ADA_TN), lambda j: (0, j)),
            pl.BlockSpec((1, ADA_TN), lambda j: (0, j)),
        ],
        out_specs=pl.BlockSpec((nb, ADA_TN), lambda j: (0, j)),
        out_shape=jax.ShapeDtypeStruct((nb, n), F32),
        compiler_params=pltpu.CompilerParams(dimension_semantics=("parallel",)),
        name="ada",
    )(c_all, w_ada, b_ada.reshape(1, n))


IN_TM = 1024
IN_TN = 2048
IN_GROUP_W = 1024
IN_ROW_CHUNK = 256
IN_NORM_ROWS = 16

TILE_POOL, TILE_QK, TILE_V_ATTN_Z, TILE_GATES = 0, 1, 2, 3


def _head_mean_square(acc):
    first_head = lax.broadcasted_iota(jnp.int32, (1, 128), 1) < HEAD_DIM
    parts = []
    for i in range(IN_GROUP_W // 128):
        sq = acc[:, i * 128:(i + 1) * 128]
        sq = sq * sq
        lo = jnp.sum(jnp.where(first_head, sq, 0.0), axis=-1, keepdims=True)
        hi = jnp.sum(jnp.where(first_head, 0.0, sq), axis=-1, keepdims=True)
        parts.append(jnp.where(first_head, lo, hi))
    return jnp.concatenate(parts, axis=1) * (1.0 / HEAD_DIM)


def _in_proj_kernel(x_ref, shift_ref, scl_ref, g_ref, w_ref, hg_ref, o_ref, h_ref):
    j = pl.program_id(1)

    def normalize(rows0):
        gm = g_ref[...] * (1.0 + scl_ref[...])
        shift = shift_ref[...]
        for s in range(IN_ROW_CHUNK // IN_NORM_ROWS):
            rows = slice(rows0 + s * IN_NORM_ROWS, rows0 + (s + 1) * IN_NORM_ROWS)
            x = x_ref[rows, :]
            ms = jnp.mean(x * x, axis=-1, keepdims=True)
            h_ref[rows, :] = (x * lax.rsqrt(ms + NORM_EPS) * gm + shift).astype(BF16)

    def run(epilogues, with_norm=False):
        for c in range(IN_TM // IN_ROW_CHUNK):
            rows = slice(c * IN_ROW_CHUNK, (c + 1) * IN_ROW_CHUNK)
            if with_norm:
                normalize(c * IN_ROW_CHUNK)
            for half, epilogue in enumerate(epilogues):
                cols = slice(half * IN_GROUP_W, (half + 1) * IN_GROUP_W)
                acc = jnp.dot(h_ref[rows, :], w_ref[:, cols], preferred_element_type=F32)
                out = epilogue(acc).astype(BF16)
                for sl in range(IN_GROUP_W // GROUP_W):
                    o_ref[half * (IN_GROUP_W // GROUP_W) + sl, rows, :] = out[:, sl * GROUP_W:(sl + 1) * GROUP_W]

    def head_norm(which):
        return lambda acc: acc * lax.rsqrt(_head_mean_square(acc) + NORM_EPS) * hg_ref[which]

    plain = lambda acc: acc

    @pl.when(j == TILE_POOL)
    def _():
        run((plain, _silu), with_norm=True)

    @pl.when(j == TILE_QK)
    def _():
        run((head_norm(0), head_norm(1)))

    @pl.when(j == TILE_V_ATTN_Z)
    def _():
        run((plain, _silu))

    @pl.when(j >= TILE_GATES)
    def _():
        run((jax.nn.sigmoid, jax.nn.sigmoid))


def _in_proj(x2, ada3, b_off, seq, norm_g, w_in_bf, head_gains):
    m = x2.shape[0]
    assert seq % IN_TM == 0 and IN_GROUP_W == ATTN_WIDTH == POOL_WIDTH
    bpb = seq // IN_TM
    return pl.pallas_call(
        _in_proj_kernel,
        grid=(m // IN_TM, IN_WIDTH // IN_TN),
        in_specs=[
            pl.BlockSpec((IN_TM, D_MODEL), lambda i, j: (i, 0)),
            pl.BlockSpec((None, 1, D_MODEL), lambda i, j: (b_off + i // bpb, 0, 0)),
            pl.BlockSpec((None, 1, D_MODEL), lambda i, j: (b_off + i // bpb, 0, 1)),
            pl.BlockSpec((1, D_MODEL), lambda i, j: (0, 0)),
            pl.BlockSpec((D_MODEL, IN_TN), lambda i, j: (0, j)),
            pl.BlockSpec((2, 1, IN_GROUP_W), lambda i, j: (0, 0, 0)),
        ],
        out_specs=pl.BlockSpec((IN_TN // GROUP_W, IN_TM, GROUP_W), lambda i, j: (j, i, 0)),
        out_shape=jax.ShapeDtypeStruct((N_SLABS, m, GROUP_W), BF16),
        scratch_shapes=[pltpu.VMEM((IN_TM, D_MODEL), BF16)],
        compiler_params=pltpu.CompilerParams(
            dimension_semantics=("parallel", "arbitrary"), vmem_limit_bytes=VMEM_LIMIT_BYTES),
        name="in_proj",
    )(x2, ada3, ada3, norm_g, w_in_bf, head_gains)


ATTN_ROWS_PER_STEP = 8
HEADS_PER_GROUP = N_HEADS // N_GROUPS
SPAN_C = 32
STRIP_C = 16
ATTN_BLOCKS = ((0, 24, 0), (24, 16, 16), (40, 24, 32))
SPAN_KEYS = WIN_R * SPAN_C
BLOCK_Q_MAX = max(nq for _, nq, _ in ATTN_BLOCKS)


def _attn_kernel(q_ref, k_ref, v_ref, z_ref, tbl_ref, o_ref, *, rows):
    lane_head = lax.broadcasted_iota(jnp.int32, (1, GROUP_W), 1) // HEAD_DIM
    head_lanes = [lane_head == h for h in range(HEADS_PER_GROUP)]

    def key_strips(ref, k0, strips):
        return jnp.concatenate(
            [ref[pl.ds(pl.multiple_of(k0 + j * GRID_W + cb * STRIP_C, STRIP_C), STRIP_C), :]
             for cb in strips for j in range(WIN_R)], axis=0)

    def body(i, carry):
        row_ctx = []
        for u in range(ATTN_ROWS_PER_STEP):
            r = i * ATTN_ROWS_PER_STEP + u
            rs = jnp.clip(r - WIN_R // 2, 0, rows - WIN_R)
            row_ctx.append((rs - r + (WIN_R - 1), pl.multiple_of(r * GRID_W, GRID_W), pl.multiple_of(rs * GRID_W, GRID_W)))

        scores = []
        for dr0, q0, k0 in row_ctx:
            qf = q_ref[pl.ds(q0, GRID_W), :].astype(F32)
            lhs = jnp.concatenate(
                [jnp.where(hl, qf[qc0:qc0 + nq], 0.0) for qc0, nq, _ in ATTN_BLOCKS for hl in head_lanes],
                axis=0).astype(BF16)
            scores.append(lax.dot_general(lhs, key_strips(k_ref, k0, range(GRID_W // STRIP_C)),
                                          (((1,), (1,)), ((), ())), preferred_element_type=F32))

        probs = []
        for (dr0, q0, k0), s_row in zip(row_ctx, scores):
            row0 = 0
            for bi, (qc0, nq, sc) in enumerate(ATTN_BLOCKS):
                lane0 = (sc // STRIP_C) * WIN_R * STRIP_C
                s = s_row[row0:row0 + HEADS_PER_GROUP * nq, lane0:lane0 + SPAN_KEYS]
                row0 += HEADS_PER_GROUP * nq
                s = s + jnp.concatenate(
                    [tbl_ref[h, dr0, bi * BLOCK_Q_MAX:bi * BLOCK_Q_MAX + nq, :] for h in range(HEADS_PER_GROUP)],
                    axis=0)
                m = jnp.max(s, axis=-1, keepdims=True)
                p = jnp.exp2(s - m)
                l = jnp.sum(p, axis=-1, keepdims=True)
                probs.append((p.astype(BF16), l))

        blocks = []
        pi = 0
        for dr0, q0, k0 in row_ctx:
            for qc0, nq, sc in ATTN_BLOCKS:
                p, l = probs[pi]
                pi += 1
                strips = (sc // STRIP_C, sc // STRIP_C + 1)
                pv = jnp.dot(p, key_strips(v_ref, k0, strips), preferred_element_type=F32) * (1.0 / l)
                ob = pv[:nq]
                for h in range(1, HEADS_PER_GROUP):
                    ob = jnp.where(head_lanes[h], pv[h * nq:(h + 1) * nq], ob)
                blocks.append(ob)

        for u, (dr0, q0, k0) in enumerate(row_ctx):
            o = jnp.concatenate(blocks[u * len(ATTN_BLOCKS):(u + 1) * len(ATTN_BLOCKS)], axis=0)
            o = o * z_ref[pl.ds(q0, GRID_W), :].astype(F32)
            o_ref[pl.ds(q0, GRID_W), :] = o.astype(BF16)
        return carry

    lax.fori_loop(0, rows // ATTN_ROWS_PER_STEP, body, 0)


def _attn(proj, bias_tbl, n_batch, seq):
    rows = seq // GRID_W
    assert rows % ATTN_ROWS_PER_STEP == 0 and rows >= WIN_R
    col = lambda off: (lambda b, g: (off // GROUP_W + g, b, 0))
    blk = (None, seq, GROUP_W)
    return pl.pallas_call(
        functools.partial(_attn_kernel, rows=rows),
        grid=(n_batch, N_GROUPS),
        in_specs=[
            pl.BlockSpec(blk, col(COL_Q)),
            pl.BlockSpec(blk, col(COL_K)),
            pl.BlockSpec(blk, col(COL_V)),
            pl.BlockSpec(blk, col(COL_ATTN_Z)),
            pl.BlockSpec((None, HEADS_PER_GROUP, WIN_R, len(ATTN_BLOCKS) * BLOCK_Q_MAX, SPAN_KEYS),
                         lambda b, g: (g, 0, 0, 0, 0)),
        ],
        out_specs=pl.BlockSpec((seq, GROUP_W), lambda b, g: (b, g)),
        out_shape=jax.ShapeDtypeStruct((n_batch * seq, ATTN_WIDTH), BF16),
        compiler_params=pltpu.CompilerParams(
            dimension_semantics=("parallel", "parallel"), vmem_limit_bytes=VMEM_LIMIT_BYTES),
        name="attn",
    )(proj, proj, proj, proj, bias_tbl)


def _bias_selectors():
    n_rel = 2 * WIN_C - 1
    n_strips = SPAN_C // STRIP_C
    sel_row = np.zeros((2 * WIN_R - 1, WIN_R, WIN_R), np.float32)
    for dr0 in range(WIN_R):
        for j in range(WIN_R):
            sel_row[dr0 + j, dr0, j] = 1.0
    sel_tile = np.zeros((WIN_R, n_rel, len(ATTN_BLOCKS), BLOCK_Q_MAX, n_strips, WIN_R, STRIP_C), np.float32)
    for bi, (qc0, nq, sc) in enumerate(ATTN_BLOCKS):
        for q in range(nq):
            c = qc0 + q
            c_start = min(max(c - WIN_C // 2, 0), GRID_W - WIN_C)
            for kc in range(c_start, c_start + WIN_C):
                strip, col = divmod(kc - sc, STRIP_C)
                for j in range(WIN_R):
                    sel_tile[j, kc - c + WIN_C - 1, bi, q, strip, j, col] = 1.0
    return sel_row.reshape(2 * WIN_R - 1, -1), sel_tile.reshape(WIN_R * n_rel, -1)


def _bias_table(rpb):
    sel_row, sel_tile = _bias_selectors()
    exact = lax.Precision.HIGHEST
    by_row = jnp.einsum('hdm,dy->hym', rpb, sel_row, precision=exact)
    by_row = by_row.reshape(N_HEADS * WIN_R, -1)
    tiles = jnp.dot(by_row, sel_tile, precision=exact)
    in_win = jnp.asarray(sel_tile.sum(axis=0) > 0)
    t = jnp.where(in_win, tiles * LOG2_E, MASK_BIAS).astype(F32)
    return t.reshape(N_GROUPS, HEADS_PER_GROUP, WIN_R, len(ATTN_BLOCKS) * BLOCK_Q_MAX, SPAN_KEYS)


POOL_CHUNK = 256
POOL_HALO = 16
POOL_SPAN = POOL_CHUNK + 2 * POOL_HALO


def _pool_kernel(u_ref, z_ref, w_ref, s_ref, o_ref, d_ref, *, seq):
    half = jnp.left_shift(1, pl.program_id(1))
    rel0 = (lax.broadcasted_iota(jnp.int32, (POOL_CHUNK, POOL_SPAN), 1)
            - lax.broadcasted_iota(jnp.int32, (POOL_CHUNK, POOL_SPAN), 0))
    bands = {}
    for ci in range(seq // POOL_CHUNK):
        t0 = ci * POOL_CHUNK
        start = min(max(t0 - POOL_HALO, 0), seq - POOL_SPAN)
        if start - t0 not in bands:
            rel = rel0 + (start - t0)
            bands[start - t0] = ((rel >= -half) & (rel < half)).astype(BF16)
        window_sum = jnp.dot(bands[start - t0], u_ref[start:start + POOL_SPAN, :], preferred_element_type=F32)
        t = t0 + lax.broadcasted_iota(jnp.int32, (POOL_CHUNK, 1), 0)
        cnt = jnp.minimum(t + half - 1, seq - 1) - jnp.maximum(t - half, 0) + 1
        d = window_sum / cnt.astype(F32) - u_ref[t0:t0 + POOL_CHUNK, :].astype(F32)
        d_ref[t0:t0 + POOL_CHUNK, :] = d.astype(BF16)
    w_mat = w_ref[...]
    scale = s_ref[...]
    for ci in range(seq // POOL_CHUNK):
        t0 = ci * POOL_CHUNK
        y = jnp.dot(d_ref[t0:t0 + POOL_CHUNK, :], w_mat, preferred_element_type=F32)
        y = y * scale * z_ref[t0:t0 + POOL_CHUNK, :].astype(F32)
        o_ref[t0:t0 + POOL_CHUNK, :] = y.astype(BF16)


def _pool(proj, pool_w_bf, pool_scale2, n_batch, seq):
    assert seq % POOL_CHUNK == 0 and seq >= POOL_SPAN and N_GROUPS == 4
    blk = (None, seq, GROUP_W)
    return pl.pallas_call(
        functools.partial(_pool_kernel, seq=seq),
        grid=(n_batch, N_GROUPS),
        in_specs=[
            pl.BlockSpec(blk, lambda b, g: (COL_POOL_U // GROUP_W + g, b, 0)),
            pl.BlockSpec(blk, lambda b, g: (COL_POOL_Z // GROUP_W + g, b, 0)),
            pl.BlockSpec((None, GROUP_W, GROUP_W), lambda b, g: (g, 0, 0)),
            pl.BlockSpec((1, GROUP_W), lambda b, g: (0, g)),
        ],
        out_specs=pl.BlockSpec((seq, GROUP_W), lambda b, g: (b, g)),
        out_shape=jax.ShapeDtypeStruct((n_batch * seq, POOL_WIDTH), BF16),
        scratch_shapes=[pltpu.VMEM((seq, GROUP_W), BF16)],
        compiler_params=pltpu.CompilerParams(
            dimension_semantics=("parallel", "parallel"), vmem_limit_bytes=VMEM_LIMIT_BYTES),
        name="pool",
    )(proj, proj, pool_w_bf, pool_scale2)


OUT_TM = 512
OUT_CHUNK = 512


def _out_proj_kernel(x_ref, py_ref, ay_ref, gp_ref, ga_ref, gate_ref, wpu_ref, wau_ref, wo_ref, o_ref, m_ref):
    py = py_ref[...]
    ay = ay_ref[...]
    for ci in range(D_MODEL // OUT_CHUNK):
        cs = slice(ci * OUT_CHUNK, (ci + 1) * OUT_CHUNK)
        pooled = jnp.dot(py, wpu_ref[:, cs], preferred_element_type=F32)
        attended = jnp.dot(ay, wau_ref[:, cs], preferred_element_type=F32)
        slabs = range(ci * (OUT_CHUNK // GROUP_W), (ci + 1) * (OUT_CHUNK // GROUP_W))
        g_pool = jnp.concatenate([gp_ref[sl] for sl in slabs], axis=1).astype(F32)
        g_attn = jnp.concatenate([ga_ref[sl] for sl in slabs], axis=1).astype(F32)
        m_ref[:, cs] = (g_pool * pooled + g_attn * attended).astype(BF16)
    merged = m_ref[...]
    for ci in range(D_MODEL // OUT_CHUNK):
        cs = slice(ci * OUT_CHUNK, (ci + 1) * OUT_CHUNK)
        y = jnp.dot(merged, wo_ref[:, cs], preferred_element_type=F32)
        o_ref[:, cs] = x_ref[:, cs] + gate_ref[:, cs] * y


def _out_proj(x2, pool_y, attn_y, proj, ada3, b_off, seq, wpu_bf, wau_bf, wo_bf):
    m = x2.shape[0]
    assert seq % OUT_TM == 0
    bpb = seq // OUT_TM
    resident = lambda shape: pl.BlockSpec(shape, lambda i: (0, 0), pipeline_mode=pl.Buffered(1))
    return pl.pallas_call(
        _out_proj_kernel,
        grid=(m // OUT_TM,),
        in_specs=[
            pl.BlockSpec((OUT_TM, D_MODEL), lambda i: (i, 0)),
            pl.BlockSpec((OUT_TM, POOL_WIDTH), lambda i: (i, 0)),
            pl.BlockSpec((OUT_TM, ATTN_WIDTH), lambda i: (i, 0)),
            pl.BlockSpec((D_MODEL // GROUP_W, OUT_TM, GROUP_W), lambda i: (COL_G_POOL // D_MODEL, i, 0)),
            pl.BlockSpec((D_MODEL // GROUP_W, OUT_TM, GROUP_W), lambda i: (COL_G_ATTN // D_MODEL, i, 0)),
            pl.BlockSpec((None, 1, D_MODEL), lambda i: (b_off + i // bpb, 0, 2)),
            resident((POOL_WIDTH, D_MODEL)),
            resident((ATTN_WIDTH, D_MODEL)),
            resident((D_MODEL, D_MODEL)),
        ],
        out_specs=pl.BlockSpec((OUT_TM, D_MODEL), lambda i: (i, 0)),
        out_shape=jax.ShapeDtypeStruct((m, D_MODEL), F32),
        scratch_shapes=[pltpu.VMEM((OUT_TM, D_MODEL), BF16)],
        compiler_params=pltpu.CompilerParams(
            dimension_semantics=("parallel",), vmem_limit_bytes=VMEM_LIMIT_BYTES),
        name="out_proj",
    )(x2, pool_y, attn_y, proj, proj, ada3, wpu_bf, wau_bf, wo_bf)


def kernel(x_prompt, x_sample, c_prompt, c_sample, w_ada, b_ada, norm_g, w_in, pool_w, pool_scale,
           q_norm_g, k_norm_g, rpb, w_pool_up, w_attn_up, w_o):
    assert w_ada.shape[0] == 1, "single-layer trunk"
    nb_prompt = x_prompt.shape[0]
    c_all = jnp.concatenate([c_prompt, c_sample], axis=0)
    ada = _ada(c_all, w_ada[0], b_ada[0])
    ada3 = ada.reshape(ada.shape[0], 1, 3 * D_MODEL)

    w_in_bf = w_in[0].astype(BF16)
    pool_w_bf = pool_w[0].astype(BF16)
    wpu_bf = w_pool_up[0].astype(BF16)
    wau_bf = w_attn_up[0].astype(BF16)
    wo_bf = w_o[0].astype(BF16)
    norm_g2 = norm_g[0].reshape(1, D_MODEL)
    pool_scale2 = pool_scale[0].reshape(1, POOL_WIDTH)
    qg_t = jnp.tile(q_norm_g[0] * (HEAD_DIM ** -0.5 * LOG2_E), N_HEADS)
    kg_t = jnp.tile(k_norm_g[0], N_HEADS)
    head_gains = jnp.stack([qg_t, kg_t]).reshape(2, 1, ATTN_WIDTH)
    bias_tbl = _bias_table(rpb[0])

    def trunk(x, b_off):
        n_batch, seq, _ = x.shape
        x2 = x.reshape(n_batch * seq, D_MODEL)
        proj = _in_proj(x2, ada3, b_off, seq, norm_g2, w_in_bf, head_gains)
        attn_y = _attn(proj, bias_tbl, n_batch, seq)
        pool_y = _pool(proj, pool_w_bf, pool_scale2, n_batch, seq)
        y = _out_proj(x2, pool_y, attn_y, proj, ada3, b_off, seq, wpu_bf, wau_bf, wo_bf)
        return y.reshape(n_batch, seq, D_MODEL)

    return (trunk(x_prompt, 0), trunk(x_sample, nb_prompt))
```

```python
import functools

import jax
import jax.numpy as jnp
import numpy as np
from jax import lax
from jax.experimental import pallas as pl
from jax.experimental.pallas import tpu as pltpu

D_MODEL = 2048
GRID_W = 64
POOL_WIDTH = 1024
GROUP_W = 256
N_GROUPS = 4
HEAD_DIM = 64
N_HEADS = 16
ATTN_WIDTH = 1024
WIN_R = 8
WIN_C = 16
IN_WIDTH = 10240
NORM_EPS = 1e-6
MASK_BIAS = -1e30
LOG2_E = 1.4426950408889634

COL_POOL_U, COL_POOL_Z, COL_Q, COL_K, COL_V, COL_ATTN_Z, COL_G_POOL, COL_G_ATTN = (
    0, 1024, 2048, 3072, 4096, 5120, 6144, 8192)
N_SLABS = IN_WIDTH // GROUP_W

VMEM_LIMIT_BYTES = 56 * 1024 * 1024

BF16 = jnp.bfloat16
F32 = jnp.float32


def _silu(x):
    return x * jax.nn.sigmoid(x)


ADA_TN = 1024


def _ada_kernel(c_ref, w_ref, b_ref, o_ref):
    c = c_ref[...]
    a = _silu(c).astype(BF16)
    o_ref[...] = jnp.dot(a, w_ref[...].astype(BF16), preferred_element_type=F32) + b_ref[...]


def _ada(c_all, w_ada, b_ada):
    nb = c_all.shape[0]
    n = w_ada.shape[1]
    return pl.pallas_call(
        _ada_kernel,
        grid=(n // ADA_TN,),
        in_specs=[
            pl.BlockSpec((nb, D_MODEL), lambda j: (0, 0)),
            pl.BlockSpec((D_MODEL, ADA_TN), lambda j: (0, j)),
            pl.BlockSpec((1, ADA_TN), lambda j: (0, j)),
        ],
        out_specs=pl.BlockSpec((nb, ADA_TN), lambda j: (0, j)),
        out_shape=jax.ShapeDtypeStruct((nb, n), F32),
        compiler_params=pltpu.CompilerParams(dimension_semantics=("parallel",)),
        name="ada",
    )(c_all, w_ada, b_ada.reshape(1, n))


IN_TM = 1024
IN_TN = 2048
IN_GROUP_W = 1024
IN_ROW_CHUNK = 256
IN_NORM_ROWS = 16

TILE_POOL, TILE_QK, TILE_V_ATTN_Z, TILE_GATES = 0, 1, 2, 3


def _head_mean_square(acc):
    first_head = lax.broadcasted_iota(jnp.int32, (1, 128), 1) < HEAD_DIM
    parts = []
    for i in range(IN_GROUP_W // 128):
        sq = acc[:, i * 128:(i + 1) * 128]
        sq = sq * sq
        lo = jnp.sum(jnp.where(first_head, sq, 0.0), axis=-1, keepdims=True)
        hi = jnp.sum(jnp.where(first_head, 0.0, sq), axis=-1, keepdims=True)
        parts.append(jnp.where(first_head, lo, hi))
    return jnp.concatenate(parts, axis=1) * (1.0 / HEAD_DIM)


def _in_proj_kernel(x_ref, shift_ref, scl_ref, g_ref, w_ref, hg_ref, o_ref, h_ref):
    j = pl.program_id(1)

    def normalize(rows0):
        gm = g_ref[...] * (1.0 + scl_ref[...])
        shift = shift_ref[...]
        for s in range(IN_ROW_CHUNK // IN_NORM_ROWS):
            rows = slice(rows0 + s * IN_NORM_ROWS, rows0 + (s + 1) * IN_NORM_ROWS)
            x = x_ref[rows, :]
            ms = jnp.mean(x * x, axis=-1, keepdims=True)
            h_ref[rows, :] = (x * lax.rsqrt(ms + NORM_EPS) * gm + shift).astype(BF16)

    def run(epilogues, with_norm=False):
        for c in range(IN_TM // IN_ROW_CHUNK):
            rows = slice(c * IN_ROW_CHUNK, (c + 1) * IN_ROW_CHUNK)
            if with_norm:
                normalize(c * IN_ROW_CHUNK)
            for half, epilogue in enumerate(epilogues):
                cols = slice(half * IN_GROUP_W, (half + 1) * IN_GROUP_W)
                acc = jnp.dot(h_ref[rows, :], w_ref[:, cols], preferred_element_type=F32)
                out = epilogue(acc).astype(BF16)
                for sl in range(IN_GROUP_W // GROUP_W):
                    o_ref[half * (IN_GROUP_W // GROUP_W) + sl, rows, :] = out[:, sl * GROUP_W:(sl + 1) * GROUP_W]

    def head_norm(which):
        return lambda acc: acc * lax.rsqrt(_head_mean_square(acc) + NORM_EPS) * hg_ref[which]

    plain = lambda acc: acc

    @pl.when(j == TILE_POOL)
    def _():
        run((plain, _silu), with_norm=True)

    @pl.when(j == TILE_QK)
    def _():
        run((head_norm(0), head_norm(1)))

    @pl.when(j == TILE_V_ATTN_Z)
    def _():
        run((plain, _silu))

    @pl.when(j >= TILE_GATES)
    def _():
        run((jax.nn.sigmoid, jax.nn.sigmoid))


def _in_proj(x2, ada3, b_off, seq, norm_g, w_in_bf, head_gains):
    m = x2.shape[0]
    assert seq % IN_TM == 0 and IN_GROUP_W == ATTN_WIDTH == POOL_WIDTH
    bpb = seq // IN_TM
    return pl.pallas_call(
        _in_proj_kernel,
        grid=(m // IN_TM, IN_WIDTH // IN_TN),
        in_specs=[
            pl.BlockSpec((IN_TM, D_MODEL), lambda i, j: (i, 0)),
            pl.BlockSpec((None, 1, D_MODEL), lambda i, j: (b_off + i // bpb, 0, 0)),
            pl.BlockSpec((None, 1, D_MODEL), lambda i, j: (b_off + i // bpb, 0, 1)),
            pl.BlockSpec((1, D_MODEL), lambda i, j: (0, 0)),
            pl.BlockSpec((D_MODEL, IN_TN), lambda i, j: (0, j)),
            pl.BlockSpec((2, 1, IN_GROUP_W), lambda i, j: (0, 0, 0)),
        ],
        out_specs=pl.BlockSpec((IN_TN // GROUP_W, IN_TM, GROUP_W), lambda i, j: (j, i, 0)),
        out_shape=jax.ShapeDtypeStruct((N_SLABS, m, GROUP_W), BF16),
        scratch_shapes=[pltpu.VMEM((IN_TM, D_MODEL), BF16)],
        compiler_params=pltpu.CompilerParams(
            dimension_semantics=("parallel", "arbitrary"), vmem_limit_bytes=VMEM_LIMIT_BYTES),
        name="in_proj",
    )(x2, ada3, ada3, norm_g, w_in_bf, head_gains)


ATTN_ROWS_PER_STEP = 16
HEADS_PER_GROUP = N_HEADS // N_GROUPS
SPAN_C = 32
STRIP_C = 16
ATTN_BLOCKS = ((0, 24, 0), (24, 16, 16), (40, 24, 32))
SPAN_KEYS = WIN_R * SPAN_C
BLOCK_Q_MAX = max(nq for _, nq, _ in ATTN_BLOCKS)


def _attn_kernel(q_ref, k_ref, v_ref, z_ref, tbl_ref, o_ref, *, rows):
    lane_head = lax.broadcasted_iota(jnp.int32, (1, GROUP_W), 1) // HEAD_DIM
    head_lanes = [lane_head == h for h in range(HEADS_PER_GROUP)]

    def key_strips(ref, k0, strips):
        return jnp.concatenate(
            [ref[pl.ds(pl.multiple_of(k0 + j * GRID_W + cb * STRIP_C, STRIP_C), STRIP_C), :]
             for cb in strips for j in range(WIN_R)], axis=0)

    def body(i, carry):
        row_ctx = []
        for u in range(ATTN_ROWS_PER_STEP):
            r = i * ATTN_ROWS_PER_STEP + u
            rs = jnp.clip(r - WIN_R // 2, 0, rows - WIN_R)
            row_ctx.append((rs - r + (WIN_R - 1), pl.multiple_of(r * GRID_W, GRID_W), pl.multiple_of(rs * GRID_W, GRID_W)))

        scores = []
        for dr0, q0, k0 in row_ctx:
            qf = q_ref[pl.ds(q0, GRID_W), :].astype(F32)
            lhs = jnp.concatenate(
                [jnp.where(hl, qf[qc0:qc0 + nq], 0.0) for qc0, nq, _ in ATTN_BLOCKS for hl in head_lanes],
                axis=0).astype(BF16)
            scores.append(lax.dot_general(lhs, key_strips(k_ref, k0, range(GRID_W // STRIP_C)),
                                          (((1,), (1,)), ((), ())), preferred_element_type=F32))

        probs = []
        for (dr0, q0, k0), s_row in zip(row_ctx, scores):
            row0 = 0
            for bi, (qc0, nq, sc) in enumerate(ATTN_BLOCKS):
                lane0 = (sc // STRIP_C) * WIN_R * STRIP_C
                s = s_row[row0:row0 + HEADS_PER_GROUP * nq, lane0:lane0 + SPAN_KEYS]
                row0 += HEADS_PER_GROUP * nq
                s = s + jnp.concatenate(
                    [tbl_ref[h, dr0, bi * BLOCK_Q_MAX:bi * BLOCK_Q_MAX + nq, :] for h in range(HEADS_PER_GROUP)],
                    axis=0)
                m = jnp.max(s, axis=-1, keepdims=True)
                p = jnp.exp2(s - m)
                l = jnp.sum(p, axis=-1, keepdims=True)
                probs.append((p.astype(BF16), l))

        blocks = []
        pi = 0
        for dr0, q0, k0 in row_ctx:
            for qc0, nq, sc in ATTN_BLOCKS:
                p, l = probs[pi]
                pi += 1
                strips = (sc // STRIP_C, sc // STRIP_C + 1)
                pv = jnp.dot(p, key_strips(v_ref, k0, strips), preferred_element_type=F32) * (1.0 / l)
                ob = pv[:nq]
                for h in range(1, HEADS_PER_GROUP):
                    ob = jnp.where(head_lanes[h], pv[h * nq:(h + 1) * nq], ob)
                blocks.append(ob)

        for u, (dr0, q0, k0) in enumerate(row_ctx):
            o = jnp.concatenate(blocks[u * len(ATTN_BLOCKS):(u + 1) * len(ATTN_BLOCKS)], axis=0)
            o = o * z_ref[pl.ds(q0, GRID_W), :].astype(F32)
            o_ref[pl.ds(q0, GRID_W), :] = o.astype(BF16)
        return carry

    lax.fori_loop(0, rows // ATTN_ROWS_PER_STEP, body, 0)


def _attn(proj, bias_tbl, n_batch, seq):
    rows = seq // GRID_W
    assert rows % ATTN_ROWS_PER_STEP == 0 and rows >= WIN_R
    col = lambda off: (lambda b, g: (off // GROUP_W + g, b, 0))
    blk = (None, seq, GROUP_W)
    return pl.pallas_call(
        functools.partial(_attn_kernel, rows=rows),
        grid=(n_batch, N_GROUPS),
        in_specs=[
            pl.BlockSpec(blk, col(COL_Q)),
            pl.BlockSpec(blk, col(COL_K)),
            pl.BlockSpec(blk, col(COL_V)),
            pl.BlockSpec(blk, col(COL_ATTN_Z)),
            pl.BlockSpec((None, HEADS_PER_GROUP, WIN_R, len(ATTN_BLOCKS) * BLOCK_Q_MAX, SPAN_KEYS),
                         lambda b, g: (g, 0, 0, 0, 0)),
        ],
        out_specs=pl.BlockSpec((seq, GROUP_W), lambda b, g: (b, g)),
        out_shape=jax.ShapeDtypeStruct((n_batch * seq, ATTN_WIDTH), BF16),
        compiler_params=pltpu.CompilerParams(
            dimension_semantics=("parallel", "parallel"), vmem_limit_bytes=VMEM_LIMIT_BYTES),
        name="attn",
    )(proj, proj, proj, proj, bias_tbl)


def _bias_selectors():
    n_rel = 2 * WIN_C - 1
    n_strips = SPAN_C // STRIP_C
    sel_row = np.zeros((2 * WIN_R - 1, WIN_R, WIN_R), np.float32)
    for dr0 in range(WIN_R):
        for j in range(WIN_R):
            sel_row[dr0 + j, dr0, j] = 1.0
    sel_tile = np.zeros((WIN_R, n_rel, len(ATTN_BLOCKS), BLOCK_Q_MAX, n_strips, WIN_R, STRIP_C), np.float32)
    for bi, (qc0, nq, sc) in enumerate(ATTN_BLOCKS):
        for q in range(nq):
            c = qc0 + q
            c_start = min(max(c - WIN_C // 2, 0), GRID_W - WIN_C)
            for kc in range(c_start, c_start + WIN_C):
                strip, col = divmod(kc - sc, STRIP_C)
                for j in range(WIN_R):
                    sel_tile[j, kc - c + WIN_C - 1, bi, q, strip, j, col] = 1.0
    return sel_row.reshape(2 * WIN_R - 1, -1), sel_tile.reshape(WIN_R * n_rel, -1)


def _bias_table(rpb):
    sel_row, sel_tile = _bias_selectors()
    exact = lax.Precision.HIGHEST
    by_row = jnp.einsum('hdm,dy->hym', rpb, sel_row, precision=exact)
    by_row = by_row.reshape(N_HEADS * WIN_R, -1)
    tiles = jnp.dot(by_row, sel_tile, precision=exact)
    in_win = jnp.asarray(sel_tile.sum(axis=0) > 0)
    t = jnp.where(in_win, tiles * LOG2_E, MASK_BIAS).astype(F32)
    return t.reshape(N_GROUPS, HEADS_PER_GROUP, WIN_R, len(ATTN_BLOCKS) * BLOCK_Q_MAX, SPAN_KEYS)


POOL_CHUNK = 256
POOL_HALO = 16
POOL_SPAN = POOL_CHUNK + 2 * POOL_HALO


def _pool_kernel(u_ref, z_ref, w_ref, s_ref, o_ref, d_ref, *, seq):
    half = jnp.left_shift(1, pl.program_id(1))
    rel0 = (lax.broadcasted_iota(jnp.int32, (POOL_CHUNK, POOL_SPAN), 1)
            - lax.broadcasted_iota(jnp.int32, (POOL_CHUNK, POOL_SPAN), 0))
    bands = {}
    for ci in range(seq // POOL_CHUNK):
        t0 = ci * POOL_CHUNK
        start = min(max(t0 - POOL_HALO, 0), seq - POOL_SPAN)
        if start - t0 not in bands:
            rel = rel0 + (start - t0)
            bands[start - t0] = ((rel >= -half) & (rel < half)).astype(BF16)
        window_sum = jnp.dot(bands[start - t0], u_ref[start:start + POOL_SPAN, :], preferred_element_type=F32)
        if 0 < ci < seq // POOL_CHUNK - 1:
            inv_cnt = 1.0 / (2 * half).astype(F32)
        else:
            t = t0 + lax.broadcasted_iota(jnp.int32, (POOL_CHUNK, 1), 0)
            cnt = jnp.minimum(t + half - 1, seq - 1) - jnp.maximum(t - half, 0) + 1
            inv_cnt = 1.0 / cnt.astype(F32)
        d = window_sum * inv_cnt - u_ref[t0:t0 + POOL_CHUNK, :].astype(F32)
        d_ref[t0:t0 + POOL_CHUNK, :] = d.astype(BF16)
    w_mat = w_ref[...]
    scale = s_ref[...]
    for ci in range(seq // POOL_CHUNK):
        t0 = ci * POOL_CHUNK
        y = jnp.dot(d_ref[t0:t0 + POOL_CHUNK, :], w_mat, preferred_element_type=F32)
        y = y * scale * z_ref[t0:t0 + POOL_CHUNK, :].astype(F32)
        o_ref[t0:t0 + POOL_CHUNK, :] = y.astype(BF16)


def _pool(proj, pool_w_bf, pool_scale2, n_batch, seq):
    assert seq % POOL_CHUNK == 0 and seq >= POOL_SPAN and N_GROUPS == 4
    blk = (None, seq, GROUP_W)
    return pl.pallas_call(
        functools.partial(_pool_kernel, seq=seq),
        grid=(n_batch, N_GROUPS),
        in_specs=[
            pl.BlockSpec(blk, lambda b, g: (COL_POOL_U // GROUP_W + g, b, 0)),
            pl.BlockSpec(blk, lambda b, g: (COL_POOL_Z // GROUP_W + g, b, 0)),
            pl.BlockSpec((None, GROUP_W, GROUP_W), lambda b, g: (g, 0, 0)),
            pl.BlockSpec((1, GROUP_W), lambda b, g: (0, g)),
        ],
        out_specs=pl.BlockSpec((seq, GROUP_W), lambda b, g: (b, g)),
        out_shape=jax.ShapeDtypeStruct((n_batch * seq, POOL_WIDTH), BF16),
        scratch_shapes=[pltpu.VMEM((seq, GROUP_W), BF16)],
        compiler_params=pltpu.CompilerParams(
            dimension_semantics=("parallel", "parallel"), vmem_limit_bytes=VMEM_LIMIT_BYTES),
        name="pool",
    )(proj, proj, pool_w_bf, pool_scale2)


OUT_TM = 512
OUT_CHUNK = 256


def _out_proj_kernel(x_ref, py_ref, ay_ref, gp_ref, ga_ref, gate_ref, wpu_ref, wau_ref, wo_ref, o_ref, m_ref):
    py = py_ref[...]
    ay = ay_ref[...]
    for ci in range(D_MODEL // OUT_CHUNK):
        cs = slice(ci * OUT_CHUNK, (ci + 1) * OUT_CHUNK)
        pooled = jnp.dot(py, wpu_ref[:, cs], preferred_element_type=F32)
        attended = jnp.dot(ay, wau_ref[:, cs], preferred_element_type=F32)
        slabs = range(ci * (OUT_CHUNK // GROUP_W), (ci + 1) * (OUT_CHUNK // GROUP_W))
        g_pool = jnp.concatenate([gp_ref[sl] for sl in slabs], axis=1).astype(F32)
        g_attn = jnp.concatenate([ga_ref[sl] for sl in slabs], axis=1).astype(F32)
        m_ref[:, cs] = (g_pool * pooled + g_attn * attended).astype(BF16)
    merged = m_ref[...]
    for ci in range(D_MODEL // OUT_CHUNK):
        cs = slice(ci * OUT_CHUNK, (ci + 1) * OUT_CHUNK)
        y = jnp.dot(merged, wo_ref[:, cs], preferred_element_type=F32)
        o_ref[:, cs] = x_ref[:, cs] + gate_ref[:, cs] * y


def _out_proj(x2, pool_y, attn_y, proj, ada3, b_off, seq, wpu_bf, wau_bf, wo_bf):
    m = x2.shape[0]
    assert seq % OUT_TM == 0
    bpb = seq // OUT_TM
    resident = lambda shape: pl.BlockSpec(shape, lambda i: (0, 0), pipeline_mode=pl.Buffered(1))
    return pl.pallas_call(
        _out_proj_kernel,
        grid=(m // OUT_TM,),
        in_specs=[
            pl.BlockSpec((OUT_TM, D_MODEL), lambda i: (i, 0)),
            pl.BlockSpec((OUT_TM, POOL_WIDTH), lambda i: (i, 0)),
            pl.BlockSpec((OUT_TM, ATTN_WIDTH), lambda i: (i, 0)),
            pl.BlockSpec((D_MODEL // GROUP_W, OUT_TM, GROUP_W), lambda i: (COL_G_POOL // D_MODEL, i, 0)),
            pl.BlockSpec((D_MODEL // GROUP_W, OUT_TM, GROUP_W), lambda i: (COL_G_ATTN // D_MODEL, i, 0)),
            pl.BlockSpec((None, 1, D_MODEL), lambda i: (b_off + i // bpb, 0, 2)),
            resident((POOL_WIDTH, D_MODEL)),
            resident((ATTN_WIDTH, D_MODEL)),
            resident((D_MODEL, D_MODEL)),
        ],
        out_specs=pl.BlockSpec((OUT_TM, D_MODEL), lambda i: (i, 0)),
        out_shape=jax.ShapeDtypeStruct((m, D_MODEL), F32),
        scratch_shapes=[pltpu.VMEM((OUT_TM, D_MODEL), BF16)],
        compiler_params=pltpu.CompilerParams(
            dimension_semantics=("parallel",), vmem_limit_bytes=VMEM_LIMIT_BYTES),
        name="out_proj",
    )(x2, pool_y, attn_y, proj, proj, ada3, wpu_bf, wau_bf, wo_bf)


def kernel(x_prompt, x_sample, c_prompt, c_sample, w_ada, b_ada, norm_g, w_in, pool_w, pool_scale,
           q_norm_g, k_norm_g, rpb, w_pool_up, w_attn_up, w_o):
    assert w_ada.shape[0] == 1, "single-layer trunk"
    nb_prompt = x_prompt.shape[0]
    c_all = jnp.concatenate([c_prompt, c_sample], axis=0)
    ada = _ada(c_all, w_ada[0], b_ada[0])
    ada3 = ada.reshape(ada.shape[0], 1, 3 * D_MODEL)

    w_in_bf = w_in[0].astype(BF16)
    pool_w_bf = pool_w[0].astype(BF16)
    wpu_bf = w_pool_up[0].astype(BF16)
    wau_bf = w_attn_up[0].astype(BF16)
    wo_bf = w_o[0].astype(BF16)
    norm_g2 = norm_g[0].reshape(1, D_MODEL)
    pool_scale2 = pool_scale[0].reshape(1, POOL_WIDTH)
    qg_t = jnp.tile(q_norm_g[0] * (HEAD_DIM ** -0.5 * LOG2_E), N_HEADS)
    kg_t = jnp.tile(k_norm_g[0], N_HEADS)
    head_gains = jnp.stack([qg_t, kg_t]).reshape(2, 1, ATTN_WIDTH)
    bias_tbl = _bias_table(rpb[0])

    def trunk(x, b_off):
        n_batch, seq, _ = x.shape
        x2 = x.reshape(n_batch * seq, D_MODEL)
        proj = _in_proj(x2, ada3, b_off, seq, norm_g2, w_in_bf, head_gains)
        attn_y = _attn(proj, bias_tbl, n_batch, seq)
        pool_y = _pool(proj, pool_w_bf, pool_scale2, n_batch, seq)
        y = _out_proj(x2, pool_y, attn_y, proj, ada3, b_off, seq, wpu_bf, wau_bf, wo_bf)
        return y.reshape(n_batch, seq, D_MODEL)

    return (trunk(x_prompt, 0), trunk(x_sample, nb_prompt))
```

```python
import functools

import jax
import jax.numpy as jnp
import numpy as np
from jax import lax
from jax.experimental import pallas as pl
from jax.experimental.pallas import tpu as pltpu

D_MODEL = 2048
GRID_W = 64
POOL_WIDTH = 1024
GROUP_W = 256
N_GROUPS = 4
HEAD_DIM = 64
N_HEADS = 16
ATTN_WIDTH = 1024
WIN_R = 8
WIN_C = 16
IN_WIDTH = 10240
NORM_EPS = 1e-6
MASK_BIAS = -1e30
LOG2_E = 1.4426950408889634

COL_POOL_U, COL_POOL_Z, COL_Q, COL_K, COL_V, COL_ATTN_Z, COL_G_POOL, COL_G_ATTN = (
    0, 1024, 2048, 3072, 4096, 5120, 6144, 8192)
N_SLABS = IN_WIDTH // GROUP_W

VMEM_LIMIT_BYTES = 56 * 1024 * 1024

BF16 = jnp.bfloat16
F32 = jnp.float32


def _silu(x):
    return x * jax.nn.sigmoid(x)


ADA_TN = 1024


def _ada_kernel(c_ref, w_ref, b_ref, o_ref):
    c = c_ref[...]
    a = _silu(c).astype(BF16)
    o_ref[...] = jnp.dot(a, w_ref[...].astype(BF16), preferred_element_type=F32) + b_ref[...]


def _ada(c_all, w_ada, b_ada):
    nb = c_all.shape[0]
    n = w_ada.shape[1]
    return pl.pallas_call(
        _ada_kernel,
        grid=(n // ADA_TN,),
        in_specs=[
            pl.BlockSpec((nb, D_MODEL), lambda j: (0, 0)),
            pl.BlockSpec((D_MODEL, ADA_TN), lambda j: (0, j)),
            pl.BlockSpec((1, ADA_TN), lambda j: (0, j)),
        ],
        out_specs=pl.BlockSpec((nb, ADA_TN), lambda j: (0, j)),
        out_shape=jax.ShapeDtypeStruct((nb, n), F32),
        compiler_params=pltpu.CompilerParams(dimension_semantics=("parallel",)),
        name="ada",
    )(c_all, w_ada, b_ada.reshape(1, n))


IN_TM = 1024
IN_TN = 2048
IN_GROUP_W = 1024
IN_ROW_CHUNK = 256
IN_NORM_ROWS = 16

TILE_POOL, TILE_QK, TILE_V_ATTN_Z, TILE_GATES = 0, 1, 2, 3


def _head_mean_square(acc):
    first_head = lax.broadcasted_iota(jnp.int32, (1, 128), 1) < HEAD_DIM
    parts = []
    for i in range(IN_GROUP_W // 128):
        sq = acc[:, i * 128:(i + 1) * 128]
        sq = sq * sq
        lo = jnp.sum(jnp.where(first_head, sq, 0.0), axis=-1, keepdims=True)
        hi = jnp.sum(jnp.where(first_head, 0.0, sq), axis=-1, keepdims=True)
        parts.append(jnp.where(first_head, lo, hi))
    return jnp.concatenate(parts, axis=1) * (1.0 / HEAD_DIM)


def _in_proj_kernel(x_ref, shift_ref, scl_ref, g_ref, w_ref, hg_ref, o_ref, h_ref):
    j = pl.program_id(1)

    def normalize(rows0):
        gm = g_ref[...] * (1.0 + scl_ref[...])
        shift = shift_ref[...]
        for s in range(IN_ROW_CHUNK // IN_NORM_ROWS):
            rows = slice(rows0 + s * IN_NORM_ROWS, rows0 + (s + 1) * IN_NORM_ROWS)
            x = x_ref[rows, :]
            ms = jnp.mean(x * x, axis=-1, keepdims=True)
            h_ref[rows, :] = (x * lax.rsqrt(ms + NORM_EPS) * gm + shift).astype(BF16)

    def run(epilogues, with_norm=False):
        for c in range(IN_TM // IN_ROW_CHUNK):
            rows = slice(c * IN_ROW_CHUNK, (c + 1) * IN_ROW_CHUNK)
            if with_norm:
                normalize(c * IN_ROW_CHUNK)
            for half, epilogue in enumerate(epilogues):
                cols = slice(half * IN_GROUP_W, (half + 1) * IN_GROUP_W)
                acc = jnp.dot(h_ref[rows, :], w_ref[:, cols], preferred_element_type=F32)
                out = epilogue(acc).astype(BF16)
                for sl in range(IN_GROUP_W // GROUP_W):
                    o_ref[half * (IN_GROUP_W // GROUP_W) + sl, rows, :] = out[:, sl * GROUP_W:(sl + 1) * GROUP_W]

    def head_norm(which):
        return lambda acc: acc * lax.rsqrt(_head_mean_square(acc) + NORM_EPS) * hg_ref[which]

    plain = lambda acc: acc

    @pl.when(j == TILE_POOL)
    def _():
        run((plain, _silu), with_norm=True)

    @pl.when(j == TILE_QK)
    def _():
        run((head_norm(0), head_norm(1)))

    @pl.when(j == TILE_V_ATTN_Z)
    def _():
        run((plain, _silu))

    @pl.when(j >= TILE_GATES)
    def _():
        run((jax.nn.sigmoid, jax.nn.sigmoid))


def _in_proj(x2, ada3, b_off, seq, norm_g, w_in_bf, head_gains):
    m = x2.shape[0]
    assert seq % IN_TM == 0 and IN_GROUP_W == ATTN_WIDTH == POOL_WIDTH
    bpb = seq // IN_TM
    return pl.pallas_call(
        _in_proj_kernel,
        grid=(m // IN_TM, IN_WIDTH // IN_TN),
        in_specs=[
            pl.BlockSpec((IN_TM, D_MODEL), lambda i, j: (i, 0)),
            pl.BlockSpec((None, 1, D_MODEL), lambda i, j: (b_off + i // bpb, 0, 0)),
            pl.BlockSpec((None, 1, D_MODEL), lambda i, j: (b_off + i // bpb, 0, 1)),
            pl.BlockSpec((1, D_MODEL), lambda i, j: (0, 0)),
            pl.BlockSpec((D_MODEL, IN_TN), lambda i, j: (0, j)),
            pl.BlockSpec((2, 1, IN_GROUP_W), lambda i, j: (0, 0, 0)),
        ],
        out_specs=pl.BlockSpec((IN_TN // GROUP_W, IN_TM, GROUP_W), lambda i, j: (j, i, 0)),
        out_shape=jax.ShapeDtypeStruct((N_SLABS, m, GROUP_W), BF16),
        scratch_shapes=[pltpu.VMEM((IN_TM, D_MODEL), BF16)],
        compiler_params=pltpu.CompilerParams(
            dimension_semantics=("parallel", "arbitrary"), vmem_limit_bytes=VMEM_LIMIT_BYTES),
        name="in_proj",
    )(x2, ada3, ada3, norm_g, w_in_bf, head_gains)


ATTN_ROWS_PER_STEP = 32
HEADS_PER_GROUP = N_HEADS // N_GROUPS
SPAN_C = 32
STRIP_C = 16
ATTN_BLOCKS = ((0, 24, 0), (24, 16, 16), (40, 24, 32))
SPAN_KEYS = WIN_R * SPAN_C
BLOCK_Q_MAX = max(nq for _, nq, _ in ATTN_BLOCKS)


def _attn_kernel(q_ref, k_ref, v_ref, z_ref, tbl_ref, o_ref, *, rows):
    lane_head = lax.broadcasted_iota(jnp.int32, (1, GROUP_W), 1) // HEAD_DIM
    head_lanes = [lane_head == h for h in range(HEADS_PER_GROUP)]

    def key_strips(ref, k0, strips):
        return jnp.concatenate(
            [ref[pl.ds(pl.multiple_of(k0 + j * GRID_W + cb * STRIP_C, STRIP_C), STRIP_C), :]
             for cb in strips for j in range(WIN_R)], axis=0)

    def body(i, carry):
        row_ctx = []
        for u in range(ATTN_ROWS_PER_STEP):
            r = i * ATTN_ROWS_PER_STEP + u
            rs = jnp.clip(r - WIN_R // 2, 0, rows - WIN_R)
            row_ctx.append((rs - r + (WIN_R - 1), pl.multiple_of(r * GRID_W, GRID_W), pl.multiple_of(rs * GRID_W, GRID_W)))

        scores = []
        for dr0, q0, k0 in row_ctx:
            qf = q_ref[pl.ds(q0, GRID_W), :].astype(F32)
            lhs = jnp.concatenate(
                [jnp.where(hl, qf[qc0:qc0 + nq], 0.0) for qc0, nq, _ in ATTN_BLOCKS for hl in head_lanes],
                axis=0).astype(BF16)
            scores.append(lax.dot_general(lhs, key_strips(k_ref, k0, range(GRID_W // STRIP_C)),
                                          (((1,), (1,)), ((), ())), preferred_element_type=F32))

        probs = []
        for (dr0, q0, k0), s_row in zip(row_ctx, scores):
            row0 = 0
            for bi, (qc0, nq, sc) in enumerate(ATTN_BLOCKS):
                lane0 = (sc // STRIP_C) * WIN_R * STRIP_C
                s = s_row[row0:row0 + HEADS_PER_GROUP * nq, lane0:lane0 + SPAN_KEYS]
                row0 += HEADS_PER_GROUP * nq
                s = s + jnp.concatenate(
                    [tbl_ref[h, dr0, bi * BLOCK_Q_MAX:bi * BLOCK_Q_MAX + nq, :] for h in range(HEADS_PER_GROUP)],
                    axis=0)
                m = jnp.max(s, axis=-1, keepdims=True)
                p = jnp.exp2(s - m)
                l = jnp.sum(p, axis=-1, keepdims=True)
                probs.append((p.astype(BF16), l))

        blocks = []
        pi = 0
        for dr0, q0, k0 in row_ctx:
            for qc0, nq, sc in ATTN_BLOCKS:
                p, l = probs[pi]
                pi += 1
                strips = (sc // STRIP_C, sc // STRIP_C + 1)
                pv = jnp.dot(p, key_strips(v_ref, k0, strips), preferred_element_type=F32) * (1.0 / l)
                ob = pv[:nq]
                for h in range(1, HEADS_PER_GROUP):
                    ob = jnp.where(head_lanes[h], pv[h * nq:(h + 1) * nq], ob)
                blocks.append(ob)

        for u, (dr0, q0, k0) in enumerate(row_ctx):
            o = jnp.concatenate(blocks[u * len(ATTN_BLOCKS):(u + 1) * len(ATTN_BLOCKS)], axis=0)
            o = o * z_ref[pl.ds(q0, GRID_W), :].astype(F32)
            o_ref[pl.ds(q0, GRID_W), :] = o.astype(BF16)
        return carry

    lax.fori_loop(0, rows // ATTN_ROWS_PER_STEP, body, 0)


def _attn(proj, bias_tbl, n_batch, seq):
    rows = seq // GRID_W
    assert rows % ATTN_ROWS_PER_STEP == 0 and rows >= WIN_R
    col = lambda off: (lambda b, g: (off // GROUP_W + g, b, 0))
    blk = (None, seq, GROUP_W)
    return pl.pallas_call(
        functools.partial(_attn_kernel, rows=rows),
        grid=(n_batch, N_GROUPS),
        in_specs=[
            pl.BlockSpec(blk, col(COL_Q)),
            pl.BlockSpec(blk, col(COL_K)),
            pl.BlockSpec(blk, col(COL_V)),
            pl.BlockSpec(blk, col(COL_ATTN_Z)),
            pl.BlockSpec((None, HEADS_PER_GROUP, WIN_R, len(ATTN_BLOCKS) * BLOCK_Q_MAX, SPAN_KEYS),
                         lambda b, g: (g, 0, 0, 0, 0)),
        ],
        out_specs=pl.BlockSpec((seq, GROUP_W), lambda b, g: (b, g)),
        out_shape=jax.ShapeDtypeStruct((n_batch * seq, ATTN_WIDTH), BF16),
        compiler_params=pltpu.CompilerParams(
            dimension_semantics=("parallel", "parallel"), vmem_limit_bytes=VMEM_LIMIT_BYTES),
        name="attn",
    )(proj, proj, proj, proj, bias_tbl)


def _bias_selectors():
    n_rel = 2 * WIN_C - 1
    n_strips = SPAN_C // STRIP_C
    sel_row = np.zeros((2 * WIN_R - 1, WIN_R, WIN_R), np.float32)
    for dr0 in range(WIN_R):
        for j in range(WIN_R):
            sel_row[dr0 + j, dr0, j] = 1.0
    sel_tile = np.zeros((WIN_R, n_rel, len(ATTN_BLOCKS), BLOCK_Q_MAX, n_strips, WIN_R, STRIP_C), np.float32)
    for bi, (qc0, nq, sc) in enumerate(ATTN_BLOCKS):
        for q in range(nq):
            c = qc0 + q
            c_start = min(max(c - WIN_C // 2, 0), GRID_W - WIN_C)
            for kc in range(c_start, c_start + WIN_C):
                strip, col = divmod(kc - sc, STRIP_C)
                for j in range(WIN_R):
                    sel_tile[j, kc - c + WIN_C - 1, bi, q, strip, j, col] = 1.0
    return sel_row.reshape(2 * WIN_R - 1, -1), sel_tile.reshape(WIN_R * n_rel, -1)


def _bias_table(rpb):
    sel_row, sel_tile = _bias_selectors()
    exact = lax.Precision.HIGHEST
    by_row = jnp.einsum('hdm,dy->hym', rpb, sel_row, precision=exact)
    by_row = by_row.reshape(N_HEADS * WIN_R, -1)
    tiles = jnp.dot(by_row, sel_tile, precision=exact)
    in_win = jnp.asarray(sel_tile.sum(axis=0) > 0)
    t = jnp.where(in_win, tiles * LOG2_E, MASK_BIAS).astype(F32)
    return t.reshape(N_GROUPS, HEADS_PER_GROUP, WIN_R, len(ATTN_BLOCKS) * BLOCK_Q_MAX, SPAN_KEYS)


POOL_CHUNK = 256
POOL_HALO = 16
POOL_SPAN = POOL_CHUNK + 2 * POOL_HALO


def _pool_kernel(u_ref, z_ref, w_ref, s_ref, o_ref, d_ref, *, seq):
    half = jnp.left_shift(1, pl.program_id(1))
    rel0 = (lax.broadcasted_iota(jnp.int32, (POOL_CHUNK, POOL_SPAN), 1)
            - lax.broadcasted_iota(jnp.int32, (POOL_CHUNK, POOL_SPAN), 0))
    bands = {}
    for ci in range(seq // POOL_CHUNK):
        t0 = ci * POOL_CHUNK
        start = min(max(t0 - POOL_HALO, 0), seq - POOL_SPAN)
        if start - t0 not in bands:
            rel = rel0 + (start - t0)
            bands[start - t0] = ((rel >= -half) & (rel < half)).astype(BF16)
        window_sum = jnp.dot(bands[start - t0], u_ref[start:start + POOL_SPAN, :], preferred_element_type=F32)
        if 0 < ci < seq // POOL_CHUNK - 1:
            inv_cnt = 1.0 / (2 * half).astype(F32)
        else:
            t = t0 + lax.broadcasted_iota(jnp.int32, (POOL_CHUNK, 1), 0)
            cnt = jnp.minimum(t + half - 1, seq - 1) - jnp.maximum(t - half, 0) + 1
            inv_cnt = 1.0 / cnt.astype(F32)
        d = window_sum * inv_cnt - u_ref[t0:t0 + POOL_CHUNK, :].astype(F32)
        d_ref[t0:t0 + POOL_CHUNK, :] = d.astype(BF16)
    w_mat = w_ref[...]
    scale = s_ref[...]
    for ci in range(seq // POOL_CHUNK):
        t0 = ci * POOL_CHUNK
        y = jnp.dot(d_ref[t0:t0 + POOL_CHUNK, :], w_mat, preferred_element_type=F32)
        y = y * scale * z_ref[t0:t0 + POOL_CHUNK, :].astype(F32)
        o_ref[t0:t0 + POOL_CHUNK, :] = y.astype(BF16)


def _pool(proj, pool_w_bf, pool_scale2, n_batch, seq):
    assert seq % POOL_CHUNK == 0 and seq >= POOL_SPAN and N_GROUPS == 4
    blk = (None, seq, GROUP_W)
    return pl.pallas_call(
        functools.partial(_pool_kernel, seq=seq),
        grid=(n_batch, N_GROUPS),
        in_specs=[
            pl.BlockSpec(blk, lambda b, g: (COL_POOL_U // GROUP_W + g, b, 0)),
            pl.BlockSpec(blk, lambda b, g: (COL_POOL_Z // GROUP_W + g, b, 0)),
            pl.BlockSpec((None, GROUP_W, GROUP_W), lambda b, g: (g, 0, 0)),
            pl.BlockSpec((1, GROUP_W), lambda b, g: (0, g)),
        ],
        out_specs=pl.BlockSpec((seq, GROUP_W), lambda b, g: (b, g)),
        out_shape=jax.ShapeDtypeStruct((n_batch * seq, POOL_WIDTH), BF16),
        scratch_shapes=[pltpu.VMEM((seq, GROUP_W), BF16)],
        compiler_params=pltpu.CompilerParams(
            dimension_semantics=("parallel", "parallel"), vmem_limit_bytes=VMEM_LIMIT_BYTES),
        name="pool",
    )(proj, proj, pool_w_bf, pool_scale2)


OUT_TM = 512
OUT_CHUNK = 256


def _out_proj_kernel(x_ref, py_ref, ay_ref, gp_ref, ga_ref, gate_ref, wpu_ref, wau_ref, wo_ref, o_ref, m_ref):
    py = py_ref[...]
    ay = ay_ref[...]
    for ci in range(D_MODEL // OUT_CHUNK):
        cs = slice(ci * OUT_CHUNK, (ci + 1) * OUT_CHUNK)
        pooled = jnp.dot(py, wpu_ref[:, cs], preferred_element_type=F32)
        attended = jnp.dot(ay, wau_ref[:, cs], preferred_element_type=F32)
        slabs = range(ci * (OUT_CHUNK // GROUP_W), (ci + 1) * (OUT_CHUNK // GROUP_W))
        g_pool = jnp.concatenate([gp_ref[sl] for sl in slabs], axis=1).astype(F32)
        g_attn = jnp.concatenate([ga_ref[sl] for sl in slabs], axis=1).astype(F32)
        m_ref[:, cs] = (g_pool * pooled + g_attn * attended).astype(BF16)
    merged = m_ref[...]
    for ci in range(D_MODEL // OUT_CHUNK):
        cs = slice(ci * OUT_CHUNK, (ci + 1) * OUT_CHUNK)
        y = jnp.dot(merged, wo_ref[:, cs], preferred_element_type=F32)
        o_ref[:, cs] = x_ref[:, cs] + gate_ref[:, cs] * y


def _out_proj(x2, pool_y, attn_y, proj, ada3, b_off, seq, wpu_bf, wau_bf, wo_bf):
    m = x2.shape[0]
    assert seq % OUT_TM == 0
    bpb = seq // OUT_TM
    resident = lambda shape: pl.BlockSpec(shape, lambda i: (0, 0), pipeline_mode=pl.Buffered(1))
    return pl.pallas_call(
        _out_proj_kernel,
        grid=(m // OUT_TM,),
        in_specs=[
            pl.BlockSpec((OUT_TM, D_MODEL), lambda i: (i, 0)),
            pl.BlockSpec((OUT_TM, POOL_WIDTH), lambda i: (i, 0)),
            pl.BlockSpec((OUT_TM, ATTN_WIDTH), lambda i: (i, 0)),
            pl.BlockSpec((D_MODEL // GROUP_W, OUT_TM, GROUP_W), lambda i: (COL_G_POOL // D_MODEL, i, 0)),
            pl.BlockSpec((D_MODEL // GROUP_W, OUT_TM, GROUP_W), lambda i: (COL_G_ATTN // D_MODEL, i, 0)),
            pl.BlockSpec((None, 1, D_MODEL), lambda i: (b_off + i // bpb, 0, 2)),
            resident((POOL_WIDTH, D_MODEL)),
            resident((ATTN_WIDTH, D_MODEL)),
            resident((D_MODEL, D_MODEL)),
        ],
        out_specs=pl.BlockSpec((OUT_TM, D_MODEL), lambda i: (i, 0)),
        out_shape=jax.ShapeDtypeStruct((m, D_MODEL), F32),
        scratch_shapes=[pltpu.VMEM((OUT_TM, D_MODEL), BF16)],
        compiler_params=pltpu.CompilerParams(
            dimension_semantics=("parallel",), vmem_limit_bytes=VMEM_LIMIT_BYTES),
        name="out_proj",
    )(x2, pool_y, attn_y, proj, proj, ada3, wpu_bf, wau_bf, wo_bf)


def kernel(x_prompt, x_sample, c_prompt, c_sample, w_ada, b_ada, norm_g, w_in, pool_w, pool_scale,
           q_norm_g, k_norm_g, rpb, w_pool_up, w_attn_up, w_o):
    assert w_ada.shape[0] == 1, "single-layer trunk"
    nb_prompt = x_prompt.shape[0]
    c_all = jnp.concatenate([c_prompt, c_sample], axis=0)
    ada = _ada(c_all, w_ada[0], b_ada[0])
    ada3 = ada.reshape(ada.shape[0], 1, 3 * D_MODEL)

    w_in_bf = w_in[0].astype(BF16)
    pool_w_bf = pool_w[0].astype(BF16)
    wpu_bf = w_pool_up[0].astype(BF16)
    wau_bf = w_attn_up[0].astype(BF16)
    wo_bf = w_o[0].astype(BF16)
    norm_g2 = norm_g[0].reshape(1, D_MODEL)
    pool_scale2 = pool_scale[0].reshape(1, POOL_WIDTH)
    qg_t = jnp.tile(q_norm_g[0] * (HEAD_DIM ** -0.5 * LOG2_E), N_HEADS)
    kg_t = jnp.tile(k_norm_g[0], N_HEADS)
    head_gains = jnp.stack([qg_t, kg_t]).reshape(2, 1, ATTN_WIDTH)
    bias_tbl = _bias_table(rpb[0])

    def trunk(x, b_off):
        n_batch, seq, _ = x.shape
        x2 = x.reshape(n_batch * seq, D_MODEL)
        proj = _in_proj(x2, ada3, b_off, seq, norm_g2, w_in_bf, head_gains)
        attn_y = _attn(proj, bias_tbl, n_batch, seq)
        pool_y = _pool(proj, pool_w_bf, pool_scale2, n_batch, seq)
        y = _out_proj(x2, pool_y, attn_y, proj, ada3, b_off, seq, wpu_bf, wau_bf, wo_bf)
        return y.reshape(n_batch, seq, D_MODEL)

    return (trunk(x_prompt, 0), trunk(x_sample, nb_prompt))
```
